```python
import jax, jax.numpy as jnp
from jax import lax
import numpy as np

D_MODEL = 1024
BATCH = 16
SEQ = 2048
DEPTH = 4

GRID_W = 64
CTX_LEN = 256
N_HEADS = 8
QK_NOPE_DIM = 64
QK_ROPE_DIM = 32
QK_DIM = QK_NOPE_DIM + QK_ROPE_DIM
V_DIM = 64
Q_LORA_RANK = 384
KV_LORA_RANK = 256
ATTN_WIDTH = N_HEADS * V_DIM
LRU_WIDTH = D_MODEL - ATTN_WIDTH
LRU_BLOCKS = 8
LRU_BLOCK_DIM = LRU_WIDTH // LRU_BLOCKS
CONV_WIDTH = 4
LRU_C = 8.0
D_FF = -(-(8 * D_MODEL) // (3 * 256)) * 256
ROPE_BASE = 10000.0
Q_BLOCK = 128
EPS = 1e-6
IN_DIM = Q_LORA_RANK + KV_LORA_RANK + QK_ROPE_DIM + 2 * LRU_WIDTH
N_MOD = 6

kernel_name = "hybrid_mla_rglru_dit_block"


def _rmsnorm(x, g):
    xf = x.astype(jnp.float32)
    y = xf * lax.rsqrt(jnp.mean(xf * xf, axis=-1, keepdims=True) + EPS)
    return (y * g.astype(jnp.float32)).astype(x.dtype)


def _modulate(h, shift, scale):
    return h * (1.0 + scale) + shift


def _rope_1d(x, pos):
    m = x.shape[-1] // 2
    inv_freq = ROPE_BASE ** (-jnp.arange(m, dtype=jnp.float32) / m)
    ang = pos.astype(jnp.float32)[:, None] * inv_freq[None, :]
    cos = jnp.cos(ang)[:, None, :]
    sin = jnp.sin(ang)[:, None, :]
    xf = x.astype(jnp.float32)
    x1, x2 = xf[..., :m], xf[..., m:]
    return jnp.concatenate([x1 * cos - x2 * sin, x2 * cos + x1 * sin], axis=-1).astype(x.dtype)


def _axial_rope(x, row, col):
    half = QK_ROPE_DIM // 2
    x_nope = x[..., :QK_NOPE_DIM]
    x_r = x[..., QK_NOPE_DIM:QK_NOPE_DIM + half]
    x_c = x[..., QK_NOPE_DIM + half:]
    return jnp.concatenate([x_nope, _rope_1d(x_r, row), _rope_1d(x_c, col)], axis=-1)


def _mla_q(c_q, q_lora_g, w_uq, q_norm_g):
    B, T, _ = c_q.shape
    q = (_rmsnorm(c_q, q_lora_g) @ w_uq).reshape(B, T, N_HEADS, QK_DIM)
    return _rmsnorm(q, q_norm_g)


def _mla_kv(c_kv, k_r, kv_lora_g, w_ukv, k_norm_g):
    B, T, _ = c_kv.shape
    kv = (_rmsnorm(c_kv, kv_lora_g) @ w_ukv).reshape(B, T, N_HEADS, QK_NOPE_DIM + V_DIM)
    k_nope, v = kv[..., :QK_NOPE_DIM], kv[..., QK_NOPE_DIM:]
    k_rope = jnp.broadcast_to(k_r[:, :, None, :], (B, T, N_HEADS, QK_ROPE_DIM))
    k = jnp.concatenate([k_nope, k_rope], axis=-1)
    return _rmsnorm(k, k_norm_g), v


def _attend(q, k, v):
    s = jnp.einsum("bqhd,bkhd->bhqk", q, k).astype(jnp.float32) * (QK_DIM ** -0.5)
    p = jax.nn.softmax(s, axis=-1).astype(v.dtype)
    return jnp.einsum("bhqk,bkhd->bqhd", p, v)


def _latent_attention(q, k, v):
    B, S, H, Dk = q.shape
    n_blk = S // Q_BLOCK
    q_blk = q.reshape(B, n_blk, Q_BLOCK, H, Dk).transpose(1, 0, 2, 3, 4)
    o = lax.map(lambda qb: _attend(qb, k, v), q_blk)
    return o.transpose(1, 0, 2, 3, 4).reshape(B, S, H, V_DIM)


def _dwconv(u, w, b):
    T = u.shape[1]
    left = (CONV_WIDTH - 1) // 2
    up = jnp.pad(u, ((0, 0), (left, CONV_WIDTH - 1 - left), (0, 0)))
    out = b
    for j in range(CONV_WIDTH):
        out = out + up[:, j:j + T] * w[j]
    return out


def _rglru_coeffs(u, w_a, b_a, w_x, b_x, lam):
    B, T, W = u.shape
    ub = u.reshape(B, T, LRU_BLOCKS, LRU_BLOCK_DIM)
    r = jax.nn.sigmoid(jnp.einsum("btnd,nde->btne", ub, w_a).reshape(B, T, W) + b_a)
    i = jax.nn.sigmoid(jnp.einsum("btnd,nde->btne", ub, w_x).reshape(B, T, W) + b_x)
    log_a = -LRU_C * r.astype(jnp.float32) * jax.nn.softplus(-lam.astype(jnp.float32))
    a = jnp.exp(log_a)
    b = jnp.sqrt(-jnp.expm1(2.0 * log_a)) * (i * u).astype(jnp.float32)
    return a, b


def _linear_scan(a, b, h0):
    b = b.at[:, 0].add(a[:, 0] * h0)

    def combine(lhs, rhs):
        return lhs[0] * rhs[0], rhs[0] * lhs[1] + rhs[1]

    _, h = lax.associative_scan(combine, (a, b), axis=1)
    return h


def _rglru_bidirectional(u_ctx, u_lat, w_a, b_a, w_x, b_x, lam):
    B, _, W = u_ctx.shape
    hcs, hls = [], []
    for d in range(2):
        flip = (lambda t: t[:, ::-1]) if d == 1 else (lambda t: t)
        a_c, b_c = _rglru_coeffs(flip(u_ctx), w_a[d], b_a[d], w_x[d], b_x[d], lam[d])
        h_c = _linear_scan(a_c, b_c, jnp.zeros((B, W), jnp.float32))
        a_l, b_l = _rglru_coeffs(flip(u_lat), w_a[d], b_a[d], w_x[d], b_x[d], lam[d])
        h_l = _linear_scan(a_l, b_l, h_c[:, -1])
        hcs.append(flip(h_c))
        hls.append(flip(h_l))
    return hcs[0] + hcs[1], hls[0] + hls[1]


def _mixer(h, hc, row, col, need_ctx, w_in, q_lora_g, w_uq, kv_lora_g, w_ukv, q_norm_g, k_norm_g,
           conv_w, conv_b, w_rg_a, b_rg_a, w_rg_x, b_rg_x, lru_lambda, w_out):
    B, S, _ = h.shape
    C = hc.shape[1]
    splits = [Q_LORA_RANK, Q_LORA_RANK + KV_LORA_RANK, Q_LORA_RANK + KV_LORA_RANK + QK_ROPE_DIM,
              Q_LORA_RANK + KV_LORA_RANK + QK_ROPE_DIM + LRU_WIDTH]
    c_q, c_kv, k_r, u, g = jnp.split(h @ w_in, splits, axis=-1)
    cc_q, cc_kv, ck_r, cu, cg = jnp.split(hc @ w_in, splits, axis=-1)

    q = _axial_rope(_mla_q(c_q, q_lora_g, w_uq, q_norm_g), row, col)
    k, v = _mla_kv(c_kv, k_r, kv_lora_g, w_ukv, k_norm_g)
    k = _axial_rope(k, row, col)
    kc, vc = _mla_kv(cc_kv, ck_r, kv_lora_g, w_ukv, k_norm_g)
    o = _latent_attention(q, jnp.concatenate([kc, k], axis=1), jnp.concatenate([vc, v], axis=1))

    u_lat = _dwconv(u, conv_w, conv_b)
    u_ctx = _dwconv(cu, conv_w, conv_b)
    hr_ctx, hr_lat = _rglru_bidirectional(u_ctx, u_lat, w_rg_a, b_rg_a, w_rg_x, b_rg_x, lru_lambda)

    y = jnp.concatenate([o.reshape(B, S, ATTN_WIDTH),
                         jax.nn.gelu(g) * hr_lat.astype(h.dtype)], axis=-1) @ w_out
    if not need_ctx:
        return y, None
    qc = _mla_q(cc_q, q_lora_g, w_uq, q_norm_g)
    oc = _attend(qc, kc, vc)
    yc = jnp.concatenate([oc.reshape(B, C, ATTN_WIDTH),
                          jax.nn.gelu(cg) * hr_ctx.astype(hc.dtype)], axis=-1) @ w_out
    return y, yc


def _swiglu(h, w_ffn_in, w_ffn_out):
    gate, up = jnp.split(h @ w_ffn_in, [D_FF], axis=-1)
    return (jax.nn.silu(gate) * up) @ w_ffn_out


def setup_inputs(seed: int = 0) -> dict:
    key = jax.random.key(seed)
    ks = jax.random.split(key, 32)
    L = DEPTH

    def nrm(k, shape, fan_in, scale=1.0):
        return jax.random.normal(k, shape, jnp.float32) * (scale * fan_in ** -0.5)

    def gain(k, shape):
        return 1.0 + 0.02 * jax.random.normal(k, shape, jnp.float32)

    def bias(k, shape):
        return 0.01 * jax.random.normal(k, shape, jnp.float32)

    a0 = jax.random.uniform(ks[24], (L, 2, LRU_WIDTH), jnp.float32, 0.9, 0.999) ** (1.0 / LRU_C)
    lru_lambda = jnp.log(a0) - jnp.log1p(-a0)
    return {
        "x": jax.random.normal(ks[0], (BATCH, SEQ, D_MODEL), jnp.float32),
        "c": jax.random.normal(ks[1], (BATCH, D_MODEL), jnp.float32),
        "ctx": jax.random.normal(ks[2], (BATCH, CTX_LEN, D_MODEL), jnp.float32),
        "c_ctx": jax.random.normal(ks[3], (D_MODEL,), jnp.float32),
        "w_ada": nrm(ks[4], (L, D_MODEL, N_MOD * D_MODEL), D_MODEL, 0.5),
        "b_ada": bias(ks[5], (L, N_MOD * D_MODEL)),
        "norm_mix_g": gain(ks[6], (L, D_MODEL)),
        "norm_ffn_g": gain(ks[7], (L, D_MODEL)),
        "w_in": nrm(ks[8], (L, D_MODEL, IN_DIM), D_MODEL),
        "q_lora_g": gain(ks[9], (L, Q_LORA_RANK)),
        "w_uq": nrm(ks[10], (L, Q_LORA_RANK, N_HEADS * QK_DIM), Q_LORA_RANK),
        "kv_lora_g": gain(ks[11], (L, KV_LORA_RANK)),
        "w_ukv": nrm(ks[12], (L, KV_LORA_RANK, N_HEADS * (QK_NOPE_DIM + V_DIM)), KV_LORA_RANK),
        "q_norm_g": gain(ks[13], (L, QK_DIM)),
        "k_norm_g": gain(ks[14], (L, QK_DIM)),
        "conv_w": nrm(ks[15], (L, CONV_WIDTH, LRU_WIDTH), CONV_WIDTH),
        "conv_b": bias(ks[16], (L, LRU_WIDTH)),
        "w_rg_a": nrm(ks[17], (L, 2, LRU_BLOCKS, LRU_BLOCK_DIM, LRU_BLOCK_DIM), LRU_BLOCK_DIM),
        "b_rg_a": bias(ks[18], (L, 2, LRU_WIDTH)),
        "w_rg_x": nrm(ks[19], (L, 2, LRU_BLOCKS, LRU_BLOCK_DIM, LRU_BLOCK_DIM), LRU_BLOCK_DIM),
        "b_rg_x": bias(ks[20], (L, 2, LRU_WIDTH)),
        "lru_lambda": lru_lambda,
        "w_out": nrm(ks[21], (L, D_MODEL, D_MODEL), D_MODEL),
        "w_ffn_in": nrm(ks[22], (L, D_MODEL, 2 * D_FF), D_MODEL),
        "w_ffn_out": nrm(ks[23], (L, D_FF, D_MODEL), D_FF),
    }


def reference(x, c, ctx, c_ctx, w_ada, b_ada, norm_mix_g, norm_ffn_g, w_in, q_lora_g, w_uq,
              kv_lora_g, w_ukv, q_norm_g, k_norm_g, conv_w, conv_b, w_rg_a, b_rg_a, w_rg_x,
              b_rg_x, lru_lambda, w_out, w_ffn_in, w_ffn_out):
    S = x.shape[1]
    ROWS = S // GRID_W
    row = jnp.repeat(jnp.arange(ROWS, dtype=jnp.int32), GRID_W)
    col = jnp.tile(jnp.arange(GRID_W, dtype=jnp.int32), ROWS)
    s_c = jax.nn.silu(c)
    s_cc = jax.nn.silu(c_ctx)
    xc = ctx
    for l in range(DEPTH):
        need_ctx = l < DEPTH - 1
        mod = (s_c @ w_ada[l] + b_ada[l])[:, None, :]
        sh1, sc1, g1, sh2, sc2, g2 = jnp.split(mod, N_MOD, axis=-1)
        mod_c = s_cc @ w_ada[l] + b_ada[l]
        csh1, csc1, cg1, csh2, csc2, cg2 = jnp.split(mod_c, N_MOD, axis=-1)

        h = _modulate(_rmsnorm(x, norm_mix_g[l]), sh1, sc1)
        hc = _modulate(_rmsnorm(xc, norm_mix_g[l]), csh1, csc1)
        y, yc = _mixer(h, hc, row, col, need_ctx, w_in[l], q_lora_g[l], w_uq[l], kv_lora_g[l],
                       w_ukv[l], q_norm_g[l], k_norm_g[l], conv_w[l], conv_b[l], w_rg_a[l],
                       b_rg_a[l], w_rg_x[l], b_rg_x[l], lru_lambda[l], w_out[l])
        x = x + g1 * y
        h = _modulate(_rmsnorm(x, norm_ffn_g[l]), sh2, sc2)
        x = x + g2 * _swiglu(h, w_ffn_in[l], w_ffn_out[l])
        if need_ctx:
            xc = xc + cg1 * yc
            hc = _modulate(_rmsnorm(xc, norm_ffn_g[l]), csh2, csc2)
            xc = xc + cg2 * _swiglu(hc, w_ffn_in[l], w_ffn_out[l])
    return x
```

```python
import functools

import jax
import jax.numpy as jnp
import numpy as np
from jax import lax
from jax.experimental import pallas as pl
from jax.experimental.pallas import tpu as pltpu

GRID_W = 64
N_HEADS = 8
QK_NOPE_DIM = 64
QK_ROPE_DIM = 32
QK_DIM = QK_NOPE_DIM + QK_ROPE_DIM
V_DIM = 64
Q_LORA_RANK = 384
KV_LORA_RANK = 256
ATTN_WIDTH = N_HEADS * V_DIM
LRU_BLOCKS = 8
CONV_WIDTH = 4
LRU_C = 8.0
ROPE_BASE = 10000.0
EPS = 1e-6
N_MOD = 6

LANES = 128
SUBLANES = 8
HEAD_PAD = LANES
HEADS_PER_STEP = 2
ROW_BLOCK = 256
LRU_CHUNK = 256
LRU_HALF = 256
FF_CHUNK = 256
MOD_ROWS_PAD = 8
VMEM_LIMIT = 56 * 1024 * 1024

F32 = jnp.float32
BF16 = jnp.bfloat16


def _params(n_axes, vmem=VMEM_LIMIT):
    return pltpu.CompilerParams(dimension_semantics=("arbitrary",) * n_axes, vmem_limit_bytes=vmem)


def _dot(a, b):
    return jnp.dot(a, b, preferred_element_type=F32)


def _rms_scale(x, n):
    return lax.rsqrt(jnp.sum(x * x, axis=-1, keepdims=True) * (1.0 / n) + EPS)


def _mod_kernel(s_ref, w_ref, b_ref, o_ref):
    s = s_ref[...]
    s = s * jax.nn.sigmoid(s)
    o_ref[...] = _dot(s.astype(BF16), w_ref[...].astype(BF16)) + b_ref[...]


def _modulation(s_rows, w_ada, b_ada):
    L, D, N = w_ada.shape
    R = s_rows.shape[0]
    tn = N // N_MOD
    return pl.pallas_call(
        _mod_kernel,
        grid=(L, N // tn),
        in_specs=[
            pl.BlockSpec((R, D), lambda l, j: (0, 0)),
            pl.BlockSpec((None, D, tn), lambda l, j: (l, 0, j)),
            pl.BlockSpec((None, 1, tn), lambda l, j: (l, 0, j)),
        ],
        out_specs=pl.BlockSpec((None, R, tn), lambda l, j: (l, 0, j)),
        out_shape=jax.ShapeDtypeStruct((L, R, N), F32),
        compiler_params=_params(2),
        name="adaln_mod",
    )(s_rows, w_ada, b_ada)


def _pre_kernel(x_ref, mod_ref, ng_ref, win_ref, qlg_ref, wuq_ref, kvlg_ref, wukv_ref, gq_ref,
                gk_ref, cos_ref, sin_ref, q_ref, k_ref, v_ref, u_ref, g_ref, *, d_lru):
    x = x_ref[...]
    d_model = x.shape[-1]
    sh1 = mod_ref[0:1, :]
    sc1 = mod_ref[1:2, :]
    h = x * _rms_scale(x, d_model) * (ng_ref[...] * (1.0 + sc1)) + sh1
    hb = h.astype(BF16)

    c0 = Q_LORA_RANK
    c1 = c0 + KV_LORA_RANK
    c2 = c1 + HEAD_PAD
    c3 = c2 + d_lru
    c4 = c3 + d_lru
    cq = _dot(hb, win_ref[:, 0:c0])
    ckv = _dot(hb, win_ref[:, c0:c1])
    kr = _dot(hb, win_ref[:, c1:c2])
    u_ref[...] = _dot(hb, win_ref[:, c2:c3])
    g_ref[...] = jax.nn.gelu(_dot(hb, win_ref[:, c3:c4]))

    cqn = (cq * _rms_scale(cq, Q_LORA_RANK) * qlg_ref[...]).astype(BF16)
    ckvn = (ckv * _rms_scale(ckv, KV_LORA_RANK) * kvlg_ref[...]).astype(BF16)
    q_raw = _dot(cqn, wuq_ref[...])
    k_raw = _dot(ckvn, wukv_ref[:, 0:N_HEADS * HEAD_PAD])
    v_ref[...] = _dot(ckvn, wukv_ref[:, N_HEADS * HEAD_PAD:]).astype(BF16)

    cos = cos_ref[...]
    sin = sin_ref[...]
    real = lax.broadcasted_iota(jnp.int32, (1, HEAD_PAD), 1) < QK_DIM

    def head_norm_rope(xh, gain):
        ss = jnp.sum(jnp.where(real, xh * xh, 0.0), axis=-1, keepdims=True)
        xn = xh * lax.rsqrt(ss * (1.0 / QK_DIM) + EPS) * gain
        return xn * cos + pltpu.roll(xn, HEAD_PAD - QK_ROPE_DIM, axis=1) * sin

    for hd in range(N_HEADS):
        sl = slice(hd * HEAD_PAD, (hd + 1) * HEAD_PAD)
        q_ref[:, sl] = head_norm_rope(q_raw[:, sl], gq_ref[...]).astype(BF16)
        k_ref[:, sl] = head_norm_rope(k_raw[:, sl] + kr, gk_ref[...]).astype(BF16)


def _pre_call(layer, x_all, mod_all, ctx_blocks, norm_g, w_in_a, q_lora_g, w_uq_a, kv_lora_g, w_ukv_a,
              gq, gk, cos_t, sin_t):
    B, T, D = x_all.shape
    tm = ROW_BLOCK
    d_lru = D - ATTN_WIDTH
    n_in = w_in_a.shape[-1]
    hp = N_HEADS * HEAD_PAD

    def wspec(shape):
        return pl.BlockSpec((None,) + shape, lambda b, t: (layer,) + (0,) * len(shape))

    row = lambda w: pl.BlockSpec((None, tm, w), lambda b, t: (b, t, 0))
    return pl.pallas_call(
        functools.partial(_pre_kernel, d_lru=d_lru),
        grid=(B, T // tm),
        in_specs=[
            row(D),
            pl.BlockSpec((None, None, None, N_MOD, D),
                         lambda b, t: (layer, b, jnp.where(t >= ctx_blocks, 1, 0), 0, 0)),
            wspec((1, D)),
            wspec((D, n_in)),
            wspec((1, Q_LORA_RANK)),
            wspec((Q_LORA_RANK, hp)),
            wspec((1, KV_LORA_RANK)),
            wspec((KV_LORA_RANK, hp + ATTN_WIDTH)),
            wspec((1, HEAD_PAD)),
            wspec((1, HEAD_PAD)),
            pl.BlockSpec((tm, HEAD_PAD), lambda b, t: (t, 0)),
            pl.BlockSpec((tm, HEAD_PAD), lambda b, t: (t, 0)),
        ],
        out_specs=[row(hp), row(hp), row(ATTN_WIDTH), row(d_lru), row(d_lru)],
        out_shape=[
            jax.ShapeDtypeStruct((B, T, hp), BF16),
            jax.ShapeDtypeStruct((B, T, hp), BF16),
            jax.ShapeDtypeStruct((B, T, ATTN_WIDTH), BF16),
            jax.ShapeDtypeStruct((B, T, d_lru), F32),
            jax.ShapeDtypeStruct((B, T, d_lru), F32),
        ],
        compiler_params=_params(2),
        name="pre_mixer",
    )(x_all, mod_all, norm_g, w_in_a, q_lora_g, w_uq_a, kv_lora_g, w_ukv_a, gq, gk, cos_t, sin_t)


def _softplus(x):
    return jnp.maximum(x, 0.0) + jnp.log1p(jnp.exp(-jnp.abs(x)))


def _lru_kernel(u_ref, g_ref, cw_ref, cb_ref, wg_ref, bg_ref, lam_ref, o_ref,
                uc_ref, hf_ref, a_ref, b_ref, *, ctx_len):
    T, W = u_ref.shape
    tc = LRU_CHUNK
    tiles = tc // SUBLANES
    cw = cw_ref[...]
    cb = cb_ref[...]

    def conv_segment(start, n):
        seg = u_ref[start:start + n, :]
        rows = lax.broadcasted_iota(jnp.int32, (n, W), 0)
        prev1 = jnp.where(rows >= 1, pltpu.roll(seg, 1, axis=0), 0.0)
        next1 = jnp.where(rows < n - 1, pltpu.roll(seg, n - 1, axis=0), 0.0)
        next2 = jnp.where(rows < n - 2, pltpu.roll(seg, n - 2, axis=0), 0.0)
        uc_ref[start:start + n, :] = (cb + prev1 * cw[0:1] + seg * cw[1:2]
                                      + next1 * cw[2:3] + next2 * cw[3:4])

    conv_segment(0, ctx_len)
    conv_segment(ctx_len, T - ctx_len)

    row_in_tile = lax.broadcasted_iota(jnp.int32, (tc, W), 0) % SUBLANES

    def coeffs(c0, d):
        uc = uc_ref[c0:c0 + tc, :]
        ub = uc.astype(BF16)
        r = jax.nn.sigmoid(_dot(ub, wg_ref[2 * d]) + bg_ref[2 * d:2 * d + 1, :])
        i = jax.nn.sigmoid(_dot(ub, wg_ref[2 * d + 1]) + bg_ref[2 * d + 1:2 * d + 2, :])
        log_a = (-LRU_C) * r * _softplus(-lam_ref[d:d + 1, :])
        a = jnp.exp(log_a)
        b = jnp.sqrt(1.0 - a * a) * (i * uc)
        return a, b

    def tile_scan(a, b, reverse):
        for s in (1, 2, 4):
            shift = tc - s if reverse else s
            ap = pltpu.roll(a, shift, axis=0)
            bp = pltpu.roll(b, shift, axis=0)
            valid = (row_in_tile < SUBLANES - s) if reverse else (row_in_tile >= s)
            b = jnp.where(valid, a * bp + b, b)
            a = jnp.where(valid, a * ap, a)
        return a, b

    def carry_scan(dst_ref, dst0, h, reverse):
        last = 0 if reverse else SUBLANES - 1

        def body(i, h):
            j = (tiles - 1 - i) if reverse else i
            r0 = pl.multiple_of(j * SUBLANES, SUBLANES)
            o = a_ref[pl.ds(r0, SUBLANES), :] * h + b_ref[pl.ds(r0, SUBLANES), :]
            dst_ref[pl.ds(dst0 + r0, SUBLANES), :] = o
            return jnp.broadcast_to(o[last:last + 1, :], (SUBLANES, W))

        return lax.fori_loop(0, tiles, body, h, unroll=8)

    n_chunks = T // tc
    ctx_chunks = ctx_len // tc

    h = jnp.zeros((SUBLANES, W), F32)
    for c in range(n_chunks):
        a, b = tile_scan(*coeffs(c * tc, 0), reverse=False)
        a_ref[...] = a
        b_ref[...] = b
        h = carry_scan(hf_ref, c * tc, h, reverse=False)

    order = list(range(ctx_chunks - 1, -1, -1)) + list(range(n_chunks - 1, ctx_chunks - 1, -1))
    h = jnp.zeros((SUBLANES, W), F32)
    for c in order:
        a, b = tile_scan(*coeffs(c * tc, 1), reverse=True)
        a_ref[...] = a
        b_ref[...] = b
        h = carry_scan(b_ref, 0, h, reverse=True)
        sl = slice(c * tc, (c + 1) * tc)
        o_ref[sl, :] = (g_ref[sl, :] * (hf_ref[sl, :] + b_ref[...])).astype(BF16)


def _lru_call(layer, u, gg, conv_w, conv_b, wg, bg, lam, ctx_len):
    B, T, W = u.shape
    wh = LRU_HALF
    seq = lambda: pl.BlockSpec((None, T, wh), lambda b, j: (b, 0, j))
    return pl.pallas_call(
        functools.partial(_lru_kernel, ctx_len=ctx_len),
        grid=(B, W // wh),
        in_specs=[
            seq(),
            seq(),
            pl.BlockSpec((None, CONV_WIDTH, wh), lambda b, j: (layer, 0, j)),
            pl.BlockSpec((None, 1, wh), lambda b, j: (layer, 0, j)),
            pl.BlockSpec((None, None, 4, wh, wh), lambda b, j: (layer, j, 0, 0, 0)),
            pl.BlockSpec((None, 4, wh), lambda b, j: (layer, 0, j)),
            pl.BlockSpec((None, 2, wh), lambda b, j: (layer, 0, j)),
        ],
        out_specs=seq(),
        out_shape=jax.ShapeDtypeStruct((B, T, W), BF16),
        scratch_shapes=[
            pltpu.VMEM((T, wh), F32),
            pltpu.VMEM((T, wh), F32),
            pltpu.VMEM((LRU_CHUNK, wh), F32),
            pltpu.VMEM((LRU_CHUNK, wh), F32),
        ],
        compiler_params=_params(2),
        name="rglru",
    )(u, gg, conv_w, conv_b, wg, bg, lam)


def _attn_kernel(q_ref, k_ref, v_ref, o_ref, *, ctx_len, first_block):
    tq = q_ref.shape[0]
    T = k_ref.shape[0]
    t = pl.program_id(2) + first_block

    def run(nk):
        outs = []
        for hd in range(HEADS_PER_STEP):
            sl = slice(hd * HEAD_PAD, (hd + 1) * HEAD_PAD)
            s = lax.dot_general(q_ref[:, sl], k_ref[0:nk, sl], (((1,), (1,)), ((), ())),
                                preferred_element_type=F32)
            m = jnp.max(s, axis=-1, keepdims=True)
            p = jnp.exp(s - m)
            denom = jnp.sum(p, axis=-1, keepdims=True)
            outs.append(_dot(p.astype(BF16), v_ref[0:nk, :]) / denom)
        lane = lax.broadcasted_iota(jnp.int32, (tq, HEADS_PER_STEP * V_DIM), 1)
        o_ref[...] = jnp.where(lane < V_DIM, outs[0], outs[1]).astype(BF16)

    is_ctx = t * tq < ctx_len

    @pl.when(is_ctx)
    def _():
        run(ctx_len)

    @pl.when(jnp.logical_not(is_ctx))
    def _():
        run(T)


def _attn_call(q, k, v, ctx_len, first_block):
    B, T, _ = q.shape
    tq = ROW_BLOCK
    qw = HEADS_PER_STEP * HEAD_PAD
    vw = HEADS_PER_STEP * V_DIM
    n_pairs = N_HEADS // HEADS_PER_STEP
    return pl.pallas_call(
        functools.partial(_attn_kernel, ctx_len=ctx_len, first_block=first_block),
        grid=(B, n_pairs, T // tq - first_block),
        in_specs=[
            pl.BlockSpec((None, tq, qw), lambda b, p, t: (b, t + first_block, p)),
            pl.BlockSpec((None, T, qw), lambda b, p, t: (b, 0, p)),
            pl.BlockSpec((None, T, vw), lambda b, p, t: (b, 0, p)),
        ],
        out_specs=pl.BlockSpec((None, tq, vw), lambda b, p, t: (b, t + first_block, p)),
        out_shape=jax.ShapeDtypeStruct((B, T, ATTN_WIDTH), BF16),
        compiler_params=_params(3),
        name="attention",
    )(q, k, v)


def _post_kernel(x_ref, o_ref, m_ref, mod_ref, wo_ref, ng_ref, wfi_ref, wfo_ref, out_ref, *, d_ff):
    x = x_ref[...]
    d_model = x.shape[-1]
    g1 = mod_ref[2:3, :]
    sh2 = mod_ref[3:4, :]
    sc2 = mod_ref[4:5, :]
    g2 = mod_ref[5:6, :]
    y = _dot(o_ref[...], wo_ref[0:ATTN_WIDTH, :]) + _dot(m_ref[...], wo_ref[ATTN_WIDTH:, :])
    x1 = x + g1 * y
    h = x1 * _rms_scale(x1, d_model) * (ng_ref[...] * (1.0 + sc2)) + sh2
    hb = h.astype(BF16)
    acc = jnp.zeros_like(x1)
    for c in range(d_ff // FF_CHUNK):
        lo = c * FF_CHUNK
        gate = _dot(hb, wfi_ref[:, lo:lo + FF_CHUNK])
        up = _dot(hb, wfi_ref[:, d_ff + lo:d_ff + lo + FF_CHUNK])
        act = (gate * jax.nn.sigmoid(gate) * up).astype(BF16)
        acc = acc + _dot(act, wfo_ref[lo:lo + FF_CHUNK, :])
    out_ref[...] = x1 + g2 * acc


def _post_call(layer, x_all, o, m, mod_all, ctx_blocks, first_block, w_out_b, norm_g, w_ffn_in_b, w_ffn_out_b):
    B, T, D = x_all.shape
    tm = ROW_BLOCK
    d_ff = w_ffn_out_b.shape[1]
    n_blocks = T // tm - first_block

    def wspec(shape):
        return pl.BlockSpec((None,) + shape, lambda b, t: (layer,) + (0,) * len(shape),
                            pipeline_mode=pl.Buffered(1))

    row = lambda w: pl.BlockSpec((None, tm, w), lambda b, t: (b, t + first_block, 0))
    return pl.pallas_call(
        functools.partial(_post_kernel, d_ff=d_ff),
        grid=(B, n_blocks),
        in_specs=[
            row(D),
            row(ATTN_WIDTH),
            row(D - ATTN_WIDTH),
            pl.BlockSpec((None, None, None, N_MOD, D),
                         lambda b, t: (layer, b, jnp.where(t + first_block >= ctx_blocks, 1, 0), 0, 0)),
            wspec((D, D)),
            wspec((1, D)),
            wspec((D, 2 * d_ff)),
            wspec((d_ff, D)),
        ],
        out_specs=pl.BlockSpec((None, tm, D), lambda b, t: (b, t, 0)),
        out_shape=jax.ShapeDtypeStruct((B, n_blocks * tm, D), F32),
        compiler_params=_params(2),
        name="post_mixer",
    )(x_all, o, m, mod_all, w_out_b, norm_g, w_ffn_in_b, w_ffn_out_b)


def _rope_partner():
    r = np.arange(QK_ROPE_DIM)
    i = r % (QK_ROPE_DIM // 2)
    m = QK_ROPE_DIM // 4
    first = i < m
    perm = np.where(first, r + m, r - m)
    sign = np.where(first, -1.0, 1.0).astype(np.float32)
    return perm, sign


def _rope_tables(ctx_len, seq):
    m = QK_ROPE_DIM // 4
    inv_freq = ROPE_BASE ** (-jnp.arange(m, dtype=F32) / m)
    pos = jnp.arange(seq, dtype=jnp.int32)
    row = (pos // GRID_W).astype(F32)
    col = (pos % GRID_W).astype(F32)
    ang_r = row[:, None] * inv_freq[None, :]
    ang_c = col[:, None] * inv_freq[None, :]
    ang = jnp.concatenate([ang_r, ang_r, ang_c, ang_c], axis=-1)
    ang = jnp.concatenate([jnp.zeros((ctx_len, QK_ROPE_DIM), F32), ang])
    T = ctx_len + seq
    cos_t = jnp.concatenate([jnp.ones((T, QK_NOPE_DIM), F32), jnp.cos(ang),
                             jnp.zeros((T, HEAD_PAD - QK_DIM), F32)], axis=-1)
    sin_t = jnp.concatenate([jnp.zeros((T, QK_NOPE_DIM), F32), jnp.sin(ang),
                             jnp.zeros((T, HEAD_PAD - QK_DIM), F32)], axis=-1)
    return cos_t, sin_t


def _head_gain(g, perm, scale):
    rope = g[:, QK_NOPE_DIM:]
    return (jnp.concatenate([g, rope[:, perm]], axis=-1) * scale)[:, None, :]


def kernel(x, c, ctx, c_ctx, w_ada, b_ada, norm_mix_g, norm_ffn_g, w_in, q_lora_g, w_uq, kv_lora_g,
           w_ukv, q_norm_g, k_norm_g, conv_w, conv_b, w_rg_a, b_rg_a, w_rg_x, b_rg_x, lru_lambda,
           w_out, w_ffn_in, w_ffn_out):
    B, S, D = x.shape
    C = ctx.shape[1]
    L = w_ada.shape[0]
    T = C + S
    d_lru = D - ATTN_WIDTH
    assert C % ROW_BLOCK == 0 and S % ROW_BLOCK == 0 and C % LRU_CHUNK == 0 and S % LRU_CHUNK == 0
    assert d_lru % LRU_HALF == 0 and w_ffn_out.shape[1] % FF_CHUNK == 0
    ctx_blocks = C // ROW_BLOCK
    perm, sign = _rope_partner()

    s0 = Q_LORA_RANK
    s1 = s0 + KV_LORA_RANK
    s2 = s1 + QK_ROPE_DIM
    w_kr = w_in[:, :, s1:s2]
    kr_block = jnp.concatenate([jnp.zeros((L, D, QK_NOPE_DIM), F32), w_kr, w_kr[:, :, perm] * sign], axis=-1)
    w_in_a = jnp.concatenate([w_in[:, :, :s1], kr_block, w_in[:, :, s2:]], axis=-1).astype(BF16)

    wq = w_uq.reshape(L, Q_LORA_RANK, N_HEADS, QK_DIM)
    wq_rope = wq[..., QK_NOPE_DIM:]
    w_uq_a = jnp.concatenate([wq, wq_rope[..., perm] * sign], axis=-1)
    w_uq_a = w_uq_a.reshape(L, Q_LORA_RANK, N_HEADS * HEAD_PAD).astype(BF16)

    wkv = w_ukv.reshape(L, KV_LORA_RANK, N_HEADS, QK_NOPE_DIM + V_DIM)
    wk = jnp.concatenate([wkv[..., :QK_NOPE_DIM],
                          jnp.zeros((L, KV_LORA_RANK, N_HEADS, HEAD_PAD - QK_NOPE_DIM), F32)], axis=-1)
    w_ukv_a = jnp.concatenate([wk.reshape(L, KV_LORA_RANK, N_HEADS * HEAD_PAD),
                               wkv[..., QK_NOPE_DIM:].reshape(L, KV_LORA_RANK, ATTN_WIDTH)],
                              axis=-1).astype(BF16)

    gq = _head_gain(q_norm_g, perm, QK_DIM ** -0.5)
    gk = _head_gain(k_norm_g, perm, 1.0)
    cos_t, sin_t = _rope_tables(C, S)

    per_half = LRU_HALF // (d_lru // LRU_BLOCKS)
    n_half = d_lru // LRU_HALF
    wg = jnp.stack([w_rg_a[:, 0], w_rg_x[:, 0], w_rg_a[:, 1], w_rg_x[:, 1]], axis=1)
    wg = wg.reshape(L, 4, n_half, per_half, d_lru // LRU_BLOCKS, d_lru // LRU_BLOCKS)
    eye = jnp.eye(per_half, dtype=F32)
    wg = jnp.einsum("lghnde,nm->lhgndme", wg, eye)
    wg = wg.reshape(L, n_half, 4, LRU_HALF, LRU_HALF).astype(BF16)
    bg = jnp.stack([b_rg_a[:, 0], b_rg_x[:, 0], b_rg_a[:, 1], b_rg_x[:, 1]], axis=1)

    w_out_b = w_out.astype(BF16)
    w_ffn_in_b = w_ffn_in.astype(BF16)
    w_ffn_out_b = w_ffn_out.astype(BF16)

    pad = (-(B + 1)) % MOD_ROWS_PAD
    s_rows = jnp.concatenate([c, c_ctx[None, :], jnp.zeros((pad, D), F32)], axis=0)
    mod = _modulation(s_rows, w_ada, b_ada[:, None, :])
    mod_lat = mod[:, :B].reshape(L, B, 1, N_MOD, D)
    mod_ctx = jnp.broadcast_to(mod[:, B].reshape(L, 1, 1, N_MOD, D), (L, B, 1, N_MOD, D))
    mod_all = jnp.concatenate([mod_ctx, mod_lat], axis=2)

    x_all = jnp.concatenate([ctx, x], axis=1)
    for l in range(L):
        last = l == L - 1
        q, k, v, u, gg = _pre_call(l, x_all, mod_all, ctx_blocks, norm_mix_g[:, None, :], w_in_a,
                                   q_lora_g[:, None, :], w_uq_a, kv_lora_g[:, None, :], w_ukv_a,
                                   gq, gk, cos_t, sin_t)
        m = _lru_call(l, u, gg, conv_w, conv_b[:, None, :], wg, bg, lru_lambda, C)
        first_block = ctx_blocks if last else 0
        o = _attn_call(q, k, v, C, first_block)
        x_all = _post_call(l, x_all, o, m, mod_all, ctx_blocks, first_block, w_out_b,
                           norm_ffn_g[:, None, :], w_ffn_in_b, w_ffn_out_b)
    return x_all
```

```python
import functools

import jax
import jax.numpy as jnp
import numpy as np
from jax import lax
from jax.experimental import pallas as pl
from jax.experimental.pallas import tpu as pltpu

GRID_W = 64
N_HEADS = 8
QK_NOPE_DIM = 64
QK_ROPE_DIM = 32
QK_DIM = QK_NOPE_DIM + QK_ROPE_DIM
V_DIM = 64
Q_LORA_RANK = 384
KV_LORA_RANK = 256
ATTN_WIDTH = N_HEADS * V_DIM
LRU_BLOCKS = 8
CONV_WIDTH = 4
LRU_C = 8.0
ROPE_BASE = 10000.0
EPS = 1e-6
N_MOD = 6

LANES = 128
SUBLANES = 8
HEAD_PAD = LANES
HEADS_PER_STEP = 2
ROW_BLOCK = 512
ATTN_Q_BLOCK = 512
ATTN_KV_CHUNK = 2048
LRU_CHUNK = 256
LRU_TILE = LANES
FF_CHUNK = 256
MOD_ROWS_PAD = 8
VMEM_LIMIT = 56 * 1024 * 1024

F32 = jnp.float32
BF16 = jnp.bfloat16


def _params(n_axes, vmem=VMEM_LIMIT):
    return pltpu.CompilerParams(dimension_semantics=("arbitrary",) * n_axes, vmem_limit_bytes=vmem)


def _dot(a, b):
    return jnp.dot(a, b, preferred_element_type=F32)


def _rms_scale(x, n):
    return lax.rsqrt(jnp.sum(x * x, axis=-1, keepdims=True) * (1.0 / n) + EPS)


def _mod_kernel(s_ref, w_ref, b_ref, o_ref):
    s = s_ref[...]
    s = s * jax.nn.sigmoid(s)
    o_ref[...] = _dot(s.astype(BF16), w_ref[...].astype(BF16)) + b_ref[...]


def _modulation(s_rows, w_ada, b_ada):
    L, D, N = w_ada.shape
    R = s_rows.shape[0]
    tn = N // N_MOD
    return pl.pallas_call(
        _mod_kernel,
        grid=(L, N // tn),
        in_specs=[
            pl.BlockSpec((R, D), lambda l, j: (0, 0)),
            pl.BlockSpec((None, D, tn), lambda l, j: (l, 0, j)),
            pl.BlockSpec((None, 1, tn), lambda l, j: (l, 0, j)),
        ],
        out_specs=pl.BlockSpec((None, R, tn), lambda l, j: (l, 0, j)),
        out_shape=jax.ShapeDtypeStruct((L, R, N), F32),
        compiler_params=_params(2),
        name="adaln_mod",
    )(s_rows, w_ada, b_ada)


def _pre_kernel(x_ref, mod_ref, ng_ref, win_ref, qlg_ref, wuq_ref, kvlg_ref, wukv_ref, gq_ref,
                gk_ref, cos_ref, sin_ref, ones_ref, q_ref, k_ref, v_ref, u_ref, g_ref, *, d_lru):
    x = x_ref[...]
    d_model = x.shape[-1]
    sh1 = mod_ref[0:1, :]
    sc1 = mod_ref[1:2, :]
    h = x * _rms_scale(x, d_model) * (ng_ref[...] * (1.0 + sc1)) + sh1
    hb = h.astype(BF16)

    c0 = Q_LORA_RANK
    c1 = c0 + KV_LORA_RANK
    c2 = c1 + HEAD_PAD
    c3 = c2 + d_lru
    c4 = c3 + d_lru
    cq = _dot(hb, win_ref[:, 0:c0])
    ckv = _dot(hb, win_ref[:, c0:c1])
    kr = _dot(hb, win_ref[:, c1:c2])
    u_ref[...] = _dot(hb, win_ref[:, c2:c3])
    g_ref[...] = jax.nn.gelu(_dot(hb, win_ref[:, c3:c4]))

    cqn = (cq * _rms_scale(cq, Q_LORA_RANK) * qlg_ref[...]).astype(BF16)
    ckvn = (ckv * _rms_scale(ckv, KV_LORA_RANK) * kvlg_ref[...]).astype(BF16)
    q_raw = _dot(cqn, wuq_ref[...])
    k_raw = _dot(ckvn, wukv_ref[:, 0:N_HEADS * HEAD_PAD])
    v_ref[...] = _dot(ckvn, wukv_ref[:, N_HEADS * HEAD_PAD:]).astype(BF16)

    cos = cos_ref[...]
    sin = sin_ref[...]
    ones = ones_ref[...]

    def pair_norm_rope(raw, gain, dst_ref, j):
        sq = raw * raw
        hi = sq.astype(BF16)
        lo = (sq - hi.astype(F32)).astype(BF16)
        ss = _dot(hi, ones) + _dot(lo, ones)
        xn = raw * lax.rsqrt(ss * (1.0 / QK_DIM) + EPS)
        for i in range(2):
            xh = xn[:, i * HEAD_PAD:(i + 1) * HEAD_PAD] * gain
            out = xh * cos + pltpu.roll(xh, HEAD_PAD - QK_ROPE_DIM, axis=1) * sin
            hd = 2 * j + i
            dst_ref[:, hd * HEAD_PAD:(hd + 1) * HEAD_PAD] = out.astype(BF16)

    kr2 = jnp.concatenate([kr, kr], axis=1)
    for j in range(N_HEADS // 2):
        sl = slice(2 * j * HEAD_PAD, 2 * (j + 1) * HEAD_PAD)
        pair_norm_rope(q_raw[:, sl], gq_ref[...], q_ref, j)
        pair_norm_rope(k_raw[:, sl] + kr2, gk_ref[...], k_ref, j)


def _pre_call(layer, x, mod, mod_row, tables_by_block, norm_g, w_in_a, q_lora_g, w_uq_a, kv_lora_g,
              w_ukv_a, gq, gk, cos_t, sin_t, head_ones):
    Bx, Tx, D = x.shape
    tm = ROW_BLOCK
    d_lru = D - ATTN_WIDTH
    n_in = w_in_a.shape[-1]
    hp = N_HEADS * HEAD_PAD

    def wspec(shape):
        return pl.BlockSpec((None,) + shape, lambda b, t: (layer,) + (0,) * len(shape))

    row = lambda w: pl.BlockSpec((None, tm, w), lambda b, t: (b, t, 0))
    table = pl.BlockSpec((tm, HEAD_PAD), (lambda b, t: (t, 0)) if tables_by_block else (lambda b, t: (0, 0)))
    return pl.pallas_call(
        functools.partial(_pre_kernel, d_lru=d_lru),
        grid=(Bx, Tx // tm),
        in_specs=[
            row(D),
            pl.BlockSpec((None, None, N_MOD, D), lambda b, t: (layer, mod_row(b), 0, 0)),
            wspec((1, D)),
            wspec((D, n_in)),
            wspec((1, Q_LORA_RANK)),
            wspec((Q_LORA_RANK, hp)),
            wspec((1, KV_LORA_RANK)),
            wspec((KV_LORA_RANK, hp + ATTN_WIDTH)),
            wspec((1, HEAD_PAD)),
            wspec((1, HEAD_PAD)),
            table,
            table,
            pl.BlockSpec((2 * HEAD_PAD, 2 * HEAD_PAD), lambda b, t: (0, 0)),
        ],
        out_specs=[row(hp), row(hp), row(ATTN_WIDTH), row(d_lru), row(d_lru)],
        out_shape=[
            jax.ShapeDtypeStruct((Bx, Tx, hp), BF16),
            jax.ShapeDtypeStruct((Bx, Tx, hp), BF16),
            jax.ShapeDtypeStruct((Bx, Tx, ATTN_WIDTH), BF16),
            jax.ShapeDtypeStruct((Bx, Tx, d_lru), F32),
            jax.ShapeDtypeStruct((Bx, Tx, d_lru), F32),
        ],
        compiler_params=_params(2),
        name="pre_mixer",
    )(x, mod, norm_g, w_in_a, q_lora_g, w_uq_a, kv_lora_g, w_ukv_a, gq, gk, cos_t, sin_t, head_ones)


def _softplus(x):
    return jnp.maximum(x, 0.0) + jnp.log1p(jnp.exp(-jnp.abs(x)))


def _lru_kernel(uc_ref, gc_ref, ul_ref, gl_ref, cw_ref, cb_ref, wg_ref, bg_ref, lam_ref,
                oc_ref, ol_ref, uec_ref, uel_ref, cv_ref, a_ref, b_ref, hs_ref, ps_ref):
    C = uc_ref.shape[0]
    S = ul_ref.shape[0]
    cw = cw_ref[...]
    cb = cb_ref[...]
    sub = lax.broadcasted_iota(jnp.int32, (SUBLANES, LANES), 0)
    segments = ((uc_ref, gc_ref, oc_ref, uec_ref, 0, C), (ul_ref, gl_ref, ol_ref, uel_ref, C, S))

    def tile_rows(row0):
        return pl.ds(pl.multiple_of(row0, SUBLANES), SUBLANES)

    for src_ref, _, _, ue_ref, start, n in segments:
        piece = n // SUBLANES

        def gather(k, _, src_ref=src_ref, ue_ref=ue_ref, piece=piece):
            ue_ref[tile_rows(SUBLANES * (k + 1)), :] = src_ref[pl.ds(k, SUBLANES, stride=piece), :]
            return 0

        lax.fori_loop(0, piece, gather, 0, unroll=8)
        last = ue_ref[SUBLANES * piece:SUBLANES * (piece + 1), :]
        first = ue_ref[SUBLANES:2 * SUBLANES, :]
        second = ue_ref[2 * SUBLANES:3 * SUBLANES, :]
        ue_ref[0:SUBLANES, :] = jnp.where(sub >= 1, pltpu.roll(last, 1, axis=0), 0.0)
        ue_ref[SUBLANES * (piece + 1):SUBLANES * (piece + 2), :] = jnp.where(
            sub < SUBLANES - 1, pltpu.roll(first, SUBLANES - 1, axis=0), 0.0)
        ue_ref[SUBLANES * (piece + 2):SUBLANES * (piece + 3), :] = jnp.where(
            sub < SUBLANES - 1, pltpu.roll(second, SUBLANES - 1, axis=0), 0.0)
        tc = min(LRU_CHUNK, n)
        for c0 in range(0, n, tc):
            taps = [ue_ref[c0 + SUBLANES * j:c0 + SUBLANES * j + tc, :] for j in range(CONV_WIDTH)]
            cv_ref[start + c0:start + c0 + tc, :] = (
                cb + taps[0] * cw[0:1] + taps[1] * cw[1:2] + taps[2] * cw[2:3] + taps[3] * cw[3:4])

    for d in range(2):
        decay = (-LRU_C) * _softplus(-lam_ref[d:d + 1, :])
        for c0 in range(0, C + S, LRU_CHUNK):
            uc = cv_ref[c0:c0 + LRU_CHUNK, :]
            ub = uc.astype(BF16)
            r = jax.nn.sigmoid(_dot(ub, wg_ref[2 * d]) + bg_ref[2 * d:2 * d + 1, :])
            i = jax.nn.sigmoid(_dot(ub, wg_ref[2 * d + 1]) + bg_ref[2 * d + 1:2 * d + 2, :])
            a = jnp.exp(decay * r)
            a_ref[d, c0:c0 + LRU_CHUNK, :] = a
            b_ref[d, c0:c0 + LRU_CHUNK, :] = jnp.sqrt(1.0 - a * a) * (i * uc)

    def scan_segment(start, n):
        piece = n // SUBLANES

        def body(i, carry):
            hf, pf, hb, pb = carry
            rf = tile_rows(start + SUBLANES * i)
            rb = tile_rows(start + SUBLANES * (piece - 1 - i))
            af = a_ref[0, rf, :]
            hf = af * hf + b_ref[0, rf, :]
            pf = af * pf
            hs_ref[0, rf, :] = hf
            ps_ref[0, rf, :] = pf
            ab = a_ref[1, rb, :]
            hb = ab * hb + b_ref[1, rb, :]
            pb = ab * pb
            hs_ref[1, rb, :] = hb
            ps_ref[1, rb, :] = pb
            return hf, pf, hb, pb

        zeros = jnp.zeros((SUBLANES, LANES), F32)
        ones = jnp.ones((SUBLANES, LANES), F32)
        return lax.fori_loop(0, piece, body, (zeros, ones, zeros, ones), unroll=8)

    def piece_inputs(h_fin, p_fin, h0, reverse):
        entering = [None] * SUBLANES
        state = h0
        for r in (range(SUBLANES - 1, -1, -1) if reverse else range(SUBLANES)):
            entering[r] = state
            state = h_fin[r:r + 1, :] + p_fin[r:r + 1, :] * state
        return entering, state

    zero_row = jnp.zeros((1, LANES), F32)
    entering = []
    h0f, h0b = zero_row, zero_row
    for _, _, _, _, start, n in segments:
        hf, pf, hb, pb = scan_segment(start, n)
        ef, h0f = piece_inputs(hf, pf, h0f, reverse=False)
        eb, h0b = piece_inputs(hb, pb, h0b, reverse=True)
        entering.append((ef, eb))

    for (_, g_ref, o_ref, _, start, n), (ef, eb) in zip(segments, entering):
        piece = n // SUBLANES
        for r in range(SUBLANES):

            def emit(i, _, g_ref=g_ref, o_ref=o_ref, start=start, piece=piece, r=r, ef=ef, eb=eb):
                halves = []
                for m in (2 * i, 2 * i + 1):
                    rows = pl.ds(start + m * (SUBLANES * SUBLANES) + r, SUBLANES, stride=SUBLANES)
                    halves.append(hs_ref[0, rows, :] + ps_ref[0, rows, :] * ef[r]
                                  + hs_ref[1, rows, :] + ps_ref[1, rows, :] * eb[r])
                out_rows = pl.ds(pl.multiple_of(r * piece + 2 * SUBLANES * i, 2 * SUBLANES), 2 * SUBLANES)
                o_ref[out_rows, :] = (g_ref[out_rows, :] * jnp.concatenate(halves, axis=0)).astype(BF16)
                return 0

            lax.fori_loop(0, piece // (2 * SUBLANES), emit, 0, unroll=2)


def _lru_call(layer, u_c, g_c, u_l, g_l, conv_w, conv_b, wg, bg, lam):
    B, C, W = u_c.shape
    S = u_l.shape[1]
    wt = LRU_TILE
    halo = (CONV_WIDTH - 1) * SUBLANES
    seq = lambda n: pl.BlockSpec((None, n, wt), lambda b, j: (b, 0, j))
    return pl.pallas_call(
        _lru_kernel,
        grid=(B, W // wt),
        in_specs=[
            seq(C), seq(C), seq(S), seq(S),
            pl.BlockSpec((None, CONV_WIDTH, wt), lambda b, j: (layer, 0, j)),
            pl.BlockSpec((None, 1, wt), lambda b, j: (layer, 0, j)),
            pl.BlockSpec((None, None, 4, wt, wt), lambda b, j: (layer, j, 0, 0, 0)),
            pl.BlockSpec((None, 4, wt), lambda b, j: (layer, 0, j)),
            pl.BlockSpec((None, 2, wt), lambda b, j: (layer, 0, j)),
        ],
        out_specs=[seq(C), seq(S)],
        out_shape=[jax.ShapeDtypeStruct((B, C, W), BF16), jax.ShapeDtypeStruct((B, S, W), BF16)],
        scratch_shapes=[
            pltpu.VMEM((C + halo, wt), F32),
            pltpu.VMEM((S + halo, wt), F32),
            pltpu.VMEM((C + S, wt), F32),
        ] + [pltpu.VMEM((2, C + S, wt), F32)] * 4,
        compiler_params=_params(2),
        name="rglru",
    )(u_c, g_c, u_l, g_l, conv_w, conv_b, wg, bg, lam)


def _attn_kernel(q_ref, *refs):
    o_ref = refs[-1]
    segments = [(refs[2 * i], refs[2 * i + 1]) for i in range((len(refs) - 1) // 2)]
    tq = q_ref.shape[0]

    def lane_partial_sum(p):
        out = p[:, 0:LANES]
        for j in range(1, p.shape[1] // LANES):
            out = out + p[:, j * LANES:(j + 1) * LANES]
        return out

    chunks = []
    for k_ref, v_ref in segments:
        n = k_ref.shape[0]
        kc = min(ATTN_KV_CHUNK, n)
        chunks += [(k_ref, v_ref, slice(c * kc, (c + 1) * kc)) for c in range(n // kc)]

    state = [None] * HEADS_PER_STEP
    for k_ref, v_ref, ks in chunks:
        for hd in range(HEADS_PER_STEP):
            sl = slice(hd * HEAD_PAD, (hd + 1) * HEAD_PAD)
            vs = slice((hd // 2) * LANES, (hd // 2 + 1) * LANES)
            s = lax.dot_general(q_ref[:, sl], k_ref[ks, sl], (((1,), (1,)), ((), ())),
                                preferred_element_type=F32)
            mc = jnp.max(s, axis=-1, keepdims=True)
            if state[hd] is None:
                m = mc
                p = jnp.exp(s - m)
                l = lane_partial_sum(p)
                acc = _dot(p.astype(BF16), v_ref[ks, vs])
            else:
                m_old, l, acc = state[hd]
                m = jnp.maximum(m_old, mc)
                alpha = jnp.exp(m_old - m)
                p = jnp.exp(s - m)
                l = alpha * l + lane_partial_sum(p)
                acc = alpha * acc + _dot(p.astype(BF16), v_ref[ks, vs])
            state[hd] = (m, l, acc)
    outs = [acc / jnp.sum(l, axis=-1, keepdims=True) for (_, l, acc) in state]
    low = lax.broadcasted_iota(jnp.int32, (tq, LANES), 1) < V_DIM
    for j in range(HEADS_PER_STEP // 2):
        o_ref[:, j * LANES:(j + 1) * LANES] = jnp.where(low, outs[2 * j], outs[2 * j + 1]).astype(BF16)


def _attn_call(q, kv_segments):
    B, Tq, _ = q.shape
    tq = min(ATTN_Q_BLOCK, Tq)
    qw = HEADS_PER_STEP * HEAD_PAD
    vw = HEADS_PER_STEP * V_DIM
    in_specs = [pl.BlockSpec((None, tq, qw), lambda b, p, t: (b, t, p))]
    args = [q]
    for k, v in kv_segments:
        n = k.shape[1]
        in_specs += [pl.BlockSpec((None, n, qw), lambda b, p, t: (b, 0, p)),
                     pl.BlockSpec((None, n, vw), lambda b, p, t: (b, 0, p))]
        args += [k, v]
    return pl.pallas_call(
        _attn_kernel,
        grid=(B, N_HEADS // HEADS_PER_STEP, Tq // tq),
        in_specs=in_specs,
        out_specs=pl.BlockSpec((None, tq, vw), lambda b, p, t: (b, t, p)),
        out_shape=jax.ShapeDtypeStruct((B, Tq, ATTN_WIDTH), BF16),
        compiler_params=_params(3),
        name="attention",
    )(*args)


def _post_kernel(x_ref, o_ref, m_ref, mod_ref, wo_ref, ng_ref, wfi_ref, wfo_ref, out_ref, *, d_ff):
    x = x_ref[...]
    d_model = x.shape[-1]
    g1 = mod_ref[2:3, :]
    sh2 = mod_ref[3:4, :]
    sc2 = mod_ref[4:5, :]
    g2 = mod_ref[5:6, :]
    y = _dot(o_ref[...], wo_ref[0:ATTN_WIDTH, :]) + _dot(m_ref[...], wo_ref[ATTN_WIDTH:, :])
    x1 = x + g1 * y
    h = x1 * _rms_scale(x1, d_model) * (ng_ref[...] * (1.0 + sc2)) + sh2
    hb = h.astype(BF16)
    acc = jnp.zeros_like(x1)
    for c in range(d_ff // FF_CHUNK):
        lo = c * FF_CHUNK
        gate = _dot(hb, wfi_ref[:, lo:lo + FF_CHUNK])
        up = _dot(hb, wfi_ref[:, d_ff + lo:d_ff + lo + FF_CHUNK])
        act = (gate * jax.nn.sigmoid(gate) * up).astype(BF16)
        acc = acc + _dot(act, wfo_ref[lo:lo + FF_CHUNK, :])
    out_ref[...] = x1 + g2 * acc


def _post_call(layer, x, o, m, mod, mod_row, w_out_b, norm_g, w_ffn_in_b, w_ffn_out_b):
    Bx, Tx, D = x.shape
    tm = ROW_BLOCK
    d_ff = w_ffn_out_b.shape[1]

    def wspec(shape):
        return pl.BlockSpec((None,) + shape, lambda b, t: (layer,) + (0,) * len(shape),
                            pipeline_mode=pl.Buffered(1))

    row = lambda w: pl.BlockSpec((None, tm, w), lambda b, t: (b, t, 0))
    return pl.pallas_call(
        functools.partial(_post_kernel, d_ff=d_ff),
        grid=(Bx, Tx // tm),
        in_specs=[
            row(D),
            row(ATTN_WIDTH),
            row(D - ATTN_WIDTH),
            pl.BlockSpec((None, None, N_MOD, D), lambda b, t: (layer, mod_row(b), 0, 0)),
            wspec((D, D)),
            wspec((1, D)),
            wspec((D, 2 * d_ff)),
            wspec((d_ff, D)),
        ],
        out_specs=row(D),
        out_shape=jax.ShapeDtypeStruct((Bx, Tx, D), F32),
        compiler_params=_params(2),
        name="post_mixer",
    )(x, o, m, mod, w_out_b, norm_g, w_ffn_in_b, w_ffn_out_b)


def _rope_partner():
    r = np.arange(QK_ROPE_DIM)
    i = r % (QK_ROPE_DIM // 2)
    m = QK_ROPE_DIM // 4
    first = i < m
    perm = np.where(first, r + m, r - m)
    sign = np.where(first, -1.0, 1.0).astype(np.float32)
    return perm, sign


def _rope_tables(ang):
    n = ang.shape[0]
    cos_t = jnp.concatenate([jnp.ones((n, QK_NOPE_DIM), F32), jnp.cos(ang),
                             jnp.zeros((n, HEAD_PAD - QK_DIM), F32)], axis=-1)
    sin_t = jnp.concatenate([jnp.zeros((n, QK_NOPE_DIM), F32), jnp.sin(ang),
                             jnp.zeros((n, HEAD_PAD - QK_DIM), F32)], axis=-1)
    return cos_t, sin_t


def _latent_angles(seq):
    m = QK_ROPE_DIM // 4
    inv_freq = ROPE_BASE ** (-jnp.arange(m, dtype=F32) / m)
    pos = jnp.arange(seq, dtype=jnp.int32)
    ang_r = (pos // GRID_W).astype(F32)[:, None] * inv_freq[None, :]
    ang_c = (pos % GRID_W).astype(F32)[:, None] * inv_freq[None, :]
    return jnp.concatenate([ang_r, ang_r, ang_c, ang_c], axis=-1)


def _head_gain(g, perm, scale):
    rope = g[:, QK_NOPE_DIM:]
    return (jnp.concatenate([g, rope[:, perm]], axis=-1) * scale)[:, None, :]


def kernel(x, c, ctx, c_ctx, w_ada, b_ada, norm_mix_g, norm_ffn_g, w_in, q_lora_g, w_uq, kv_lora_g,
           w_ukv, q_norm_g, k_norm_g, conv_w, conv_b, w_rg_a, b_rg_a, w_rg_x, b_rg_x, lru_lambda,
           w_out, w_ffn_in, w_ffn_out):
    B, S, D = x.shape
    C = ctx.shape[1]
    L = w_ada.shape[0]
    d_lru = D - ATTN_WIDTH
    assert S % ROW_BLOCK == 0 and (B * C) % ROW_BLOCK == 0 and S % ATTN_Q_BLOCK == 0
    assert C % LRU_CHUNK == 0 and S % LRU_CHUNK == 0 and S % min(ATTN_KV_CHUNK, S) == 0
    assert d_lru % LRU_TILE == 0 and w_ffn_out.shape[1] % FF_CHUNK == 0
    perm, sign = _rope_partner()

    s0 = Q_LORA_RANK
    s1 = s0 + KV_LORA_RANK
    s2 = s1 + QK_ROPE_DIM
    w_kr = w_in[:, :, s1:s2]
    kr_block = jnp.concatenate([jnp.zeros((L, D, QK_NOPE_DIM), F32), w_kr, w_kr[:, :, perm] * sign], axis=-1)
    w_in_a = jnp.concatenate([w_in[:, :, :s1], kr_block, w_in[:, :, s2:]], axis=-1).astype(BF16)

    wq = w_uq.reshape(L, Q_LORA_RANK, N_HEADS, QK_DIM)
    wq_rope = wq[..., QK_NOPE_DIM:]
    w_uq_a = jnp.concatenate([wq, wq_rope[..., perm] * sign], axis=-1)
    w_uq_a = w_uq_a.reshape(L, Q_LORA_RANK, N_HEADS * HEAD_PAD).astype(BF16)

    wkv = w_ukv.reshape(L, KV_LORA_RANK, N_HEADS, QK_NOPE_DIM + V_DIM)
    wk = jnp.concatenate([wkv[..., :QK_NOPE_DIM],
                          jnp.zeros((L, KV_LORA_RANK, N_HEADS, HEAD_PAD - QK_NOPE_DIM), F32)], axis=-1)
    w_ukv_a = jnp.concatenate([wk.reshape(L, KV_LORA_RANK, N_HEADS * HEAD_PAD),
                               wkv[..., QK_NOPE_DIM:].reshape(L, KV_LORA_RANK, ATTN_WIDTH)],
                              axis=-1).astype(BF16)

    gq = _head_gain(q_norm_g, perm, QK_DIM ** -0.5)
    gk = _head_gain(k_norm_g, perm, 1.0)
    cos_l, sin_l = _rope_tables(_latent_angles(S))
    cos_c, sin_c = _rope_tables(jnp.zeros((ROW_BLOCK, QK_ROPE_DIM), F32))

    idx = np.arange(2 * HEAD_PAD)
    same_head = (idx[:, None] // HEAD_PAD) == (idx[None, :] // HEAD_PAD)
    head_ones = jnp.asarray(same_head & ((idx[:, None] % HEAD_PAD) < QK_DIM), BF16)

    per_half = LRU_TILE // (d_lru // LRU_BLOCKS)
    n_half = d_lru // LRU_TILE
    wg = jnp.stack([w_rg_a[:, 0], w_rg_x[:, 0], w_rg_a[:, 1], w_rg_x[:, 1]], axis=1)
    wg = wg.reshape(L, 4, n_half, per_half, d_lru // LRU_BLOCKS, d_lru // LRU_BLOCKS)
    eye = jnp.eye(per_half, dtype=F32)
    wg = jnp.einsum("lghnde,nm->lhgndme", wg, eye)
    wg = wg.reshape(L, n_half, 4, LRU_TILE, LRU_TILE).astype(BF16)
    bg = jnp.stack([b_rg_a[:, 0], b_rg_x[:, 0], b_rg_a[:, 1], b_rg_x[:, 1]], axis=1)

    w_out_b = w_out.astype(BF16)
    w_ffn_in_b = w_ffn_in.astype(BF16)
    w_ffn_out_b = w_ffn_out.astype(BF16)

    pad = (-(B + 1)) % MOD_ROWS_PAD
    s_rows = jnp.concatenate([c, c_ctx[None, :], jnp.zeros((pad, D), F32)], axis=0)
    mod = _modulation(s_rows, w_ada, b_ada[:, None, :])
    mod = mod.reshape(L, B + 1 + pad, N_MOD, D)
    lat_row = lambda b: b
    ctx_row = lambda b: B

    mix_g = norm_mix_g[:, None, :]
    ffn_g = norm_ffn_g[:, None, :]
    qlg = q_lora_g[:, None, :]
    kvlg = kv_lora_g[:, None, :]
    cb = conv_b[:, None, :]
    flat = lambda a: a.reshape(1, B * C, a.shape[-1])
    unflat = lambda a: a.reshape(B, C, a.shape[-1])

    xc = ctx
    for l in range(L):
        last = l == L - 1
        q_l, k_l, v_l, u_l, g_l = _pre_call(l, x, mod, lat_row, True, mix_g, w_in_a, qlg, w_uq_a, kvlg,
                                            w_ukv_a, gq, gk, cos_l, sin_l, head_ones)
        q_c, k_c, v_c, u_c, g_c = _pre_call(l, flat(xc), mod, ctx_row, False, mix_g, w_in_a, qlg, w_uq_a,
                                            kvlg, w_ukv_a, gq, gk, cos_c, sin_c, head_ones)
        k_c, v_c = unflat(k_c), unflat(v_c)
        m_c, m_l = _lru_call(l, unflat(u_c), unflat(g_c), u_l, g_l, conv_w, cb, wg, bg, lru_lambda)
        o_l = _attn_call(q_l, [(k_c, v_c), (k_l, v_l)])
        x = _post_call(l, x, o_l, m_l, mod, lat_row, w_out_b, ffn_g, w_ffn_in_b, w_ffn_out_b)
        if not last:
            o_c = _attn_call(unflat(q_c), [(k_c, v_c)])
            xc = unflat(_post_call(l, flat(xc), flat(o_c), flat(m_c), mod, ctx_row, w_out_b, ffn_g,
                                   w_ffn_in_b, w_ffn_out_b))
    return x
```

```python
import functools

import jax
import jax.numpy as jnp
import numpy as np
from jax import lax
from jax.experimental import pallas as pl
from jax.experimental.pallas import tpu as pltpu

GRID_W = 64
N_HEADS = 8
QK_NOPE_DIM = 64
QK_ROPE_DIM = 32
QK_DIM = QK_NOPE_DIM + QK_ROPE_DIM
V_DIM = 64
Q_LORA_RANK = 384
KV_LORA_RANK = 256
ATTN_WIDTH = N_HEADS * V_DIM
LRU_BLOCKS = 8
CONV_WIDTH = 4
LRU_C = 8.0
ROPE_BASE = 10000.0
EPS = 1e-6
N_MOD = 6
LOG2_E = 1.4426950408889634

LANES = 128
SUBLANES = 8
HEAD_PAD = LANES
HEADS_PER_STEP = 8
ROW_BLOCK = 512
ATTN_Q_BLOCK = 512
ATTN_KV_CHUNK = 2048
LRU_CHUNK = 256
LRU_TILE = LANES
FF_CHUNK = 256
MOD_ROWS_PAD = 8
VMEM_LIMIT = 56 * 1024 * 1024

F32 = jnp.float32
BF16 = jnp.bfloat16


def _params(n_axes, vmem=VMEM_LIMIT):
    return pltpu.CompilerParams(dimension_semantics=("arbitrary",) * n_axes, vmem_limit_bytes=vmem)


def _dot(a, b):
    return jnp.dot(a, b, preferred_element_type=F32)


def _rms_scale(x, n):
    return lax.rsqrt(jnp.sum(x * x, axis=-1, keepdims=True) * (1.0 / n) + EPS)


def _mod_kernel(s_ref, w_ref, b_ref, o_ref):
    s = s_ref[...]
    s = s * jax.nn.sigmoid(s)
    o_ref[...] = _dot(s.astype(BF16), w_ref[...].astype(BF16)) + b_ref[...]


def _modulation(s_rows, w_ada, b_ada):
    L, D, N = w_ada.shape
    R = s_rows.shape[0]
    tn = N // N_MOD
    return pl.pallas_call(
        _mod_kernel,
        grid=(L, N // tn),
        in_specs=[
            pl.BlockSpec((R, D), lambda l, j: (0, 0)),
            pl.BlockSpec((None, D, tn), lambda l, j: (l, 0, j)),
            pl.BlockSpec((None, 1, tn), lambda l, j: (l, 0, j)),
        ],
        out_specs=pl.BlockSpec((None, R, tn), lambda l, j: (l, 0, j)),
        out_shape=jax.ShapeDtypeStruct((L, R, N), F32),
        compiler_params=_params(2),
        name="adaln_mod",
    )(s_rows, w_ada, b_ada)


def _pre_kernel(x_ref, mod_ref, ng_ref, win_ref, qlg_ref, wuq_ref, kvlg_ref, wukv_ref, gq_ref,
                gk_ref, cos_ref, sin_ref, ones_ref, vone_ref, q_ref, k_ref, v_ref, u_ref, g_ref, *, d_lru):
    x = x_ref[...]
    d_model = x.shape[-1]
    sh1 = mod_ref[0:1, :]
    sc1 = mod_ref[1:2, :]
    h = x * _rms_scale(x, d_model) * (ng_ref[...] * (1.0 + sc1)) + sh1
    hb = h.astype(BF16)

    c0 = Q_LORA_RANK
    c1 = c0 + KV_LORA_RANK
    c2 = c1 + HEAD_PAD
    c3 = c2 + d_lru
    c4 = c3 + d_lru
    cq = _dot(hb, win_ref[:, 0:c0])
    ckv = _dot(hb, win_ref[:, c0:c1])
    kr = _dot(hb, win_ref[:, c1:c2])
    u_ref[...] = _dot(hb, win_ref[:, c2:c3])
    g_ref[...] = jax.nn.gelu(_dot(hb, win_ref[:, c3:c4]))

    cqn = (cq * _rms_scale(cq, Q_LORA_RANK) * qlg_ref[...]).astype(BF16)
    ckvn = (ckv * _rms_scale(ckv, KV_LORA_RANK) * kvlg_ref[...]).astype(BF16)
    q_raw = _dot(cqn, wuq_ref[...])
    k_raw = _dot(ckvn, wukv_ref[:, 0:N_HEADS * HEAD_PAD])
    v_ref[...] = (_dot(ckvn, wukv_ref[:, N_HEADS * HEAD_PAD:]) + vone_ref[...]).astype(BF16)

    cos = cos_ref[...]
    sin = sin_ref[...]
    ones = ones_ref[...]

    def pair_norm_rope(raw, gain, dst_ref, j):
        sq = raw * raw
        hi = sq.astype(BF16)
        lo = (sq - hi.astype(F32)).astype(BF16)
        ss = _dot(hi, ones) + _dot(lo, ones)
        xn = raw * lax.rsqrt(ss * (1.0 / QK_DIM) + EPS)
        for i in range(2):
            xh = xn[:, i * HEAD_PAD:(i + 1) * HEAD_PAD] * gain
            out = xh * cos + pltpu.roll(xh, HEAD_PAD - QK_ROPE_DIM, axis=1) * sin
            hd = 2 * j + i
            dst_ref[:, hd * HEAD_PAD:(hd + 1) * HEAD_PAD] = out.astype(BF16)

    kr2 = jnp.concatenate([kr, kr], axis=1)
    for j in range(N_HEADS // 2):
        sl = slice(2 * j * HEAD_PAD, 2 * (j + 1) * HEAD_PAD)
        pair_norm_rope(q_raw[:, sl], gq_ref[...], q_ref, j)
        pair_norm_rope(k_raw[:, sl] + kr2, gk_ref[...], k_ref, j)


def _pre_call(layer, x, mod, mod_row, tables_by_block, norm_g, w_in_a, q_lora_g, w_uq_a, kv_lora_g,
              w_ukv_a, gq, gk, cos_t, sin_t, head_ones, v_ones):
    Bx, Tx, D = x.shape
    tm = ROW_BLOCK
    d_lru = D - ATTN_WIDTH
    n_in = w_in_a.shape[-1]
    hp = N_HEADS * HEAD_PAD

    def wspec(shape):
        return pl.BlockSpec((None,) + shape, lambda b, t: (layer,) + (0,) * len(shape))

    row = lambda w: pl.BlockSpec((None, tm, w), lambda b, t: (b, t, 0))
    table = pl.BlockSpec((tm, HEAD_PAD), (lambda b, t: (t, 0)) if tables_by_block else (lambda b, t: (0, 0)))
    return pl.pallas_call(
        functools.partial(_pre_kernel, d_lru=d_lru),
        grid=(Bx, Tx // tm),
        in_specs=[
            row(D),
            pl.BlockSpec((None, None, N_MOD, D), lambda b, t: (layer, mod_row(b), 0, 0)),
            wspec((1, D)),
            wspec((D, n_in)),
            wspec((1, Q_LORA_RANK)),
            wspec((Q_LORA_RANK, hp)),
            wspec((1, KV_LORA_RANK)),
            wspec((KV_LORA_RANK, 2 * hp)),
            wspec((1, HEAD_PAD)),
            wspec((1, HEAD_PAD)),
            table,
            table,
            pl.BlockSpec((2 * HEAD_PAD, 2 * HEAD_PAD), lambda b, t: (0, 0)),
            pl.BlockSpec((1, hp), lambda b, t: (0, 0)),
        ],
        out_specs=[row(hp), row(hp), row(hp), row(d_lru), row(d_lru)],
        out_shape=[
            jax.ShapeDtypeStruct((Bx, Tx, hp), BF16),
            jax.ShapeDtypeStruct((Bx, Tx, hp), BF16),
            jax.ShapeDtypeStruct((Bx, Tx, hp), BF16),
            jax.ShapeDtypeStruct((Bx, Tx, d_lru), F32),
            jax.ShapeDtypeStruct((Bx, Tx, d_lru), F32),
        ],
        compiler_params=_params(2),
        name="pre_mixer",
    )(x, mod, norm_g, w_in_a, q_lora_g, w_uq_a, kv_lora_g, w_ukv_a, gq, gk, cos_t, sin_t, head_ones, v_ones)


def _softplus(x):
    return jnp.maximum(x, 0.0) + jnp.log1p(jnp.exp(-jnp.abs(x)))


def _lru_kernel(uc_ref, gc_ref, ul_ref, gl_ref, cw_ref, cb_ref, wg_ref, bg_ref, lam_ref,
                oc_ref, ol_ref, uec_ref, uel_ref, cv_ref, a_ref, b_ref, hs_ref, ps_ref):
    C = uc_ref.shape[0]
    S = ul_ref.shape[0]
    cw = cw_ref[...]
    cb = cb_ref[...]
    sub = lax.broadcasted_iota(jnp.int32, (SUBLANES, LANES), 0)
    segments = ((uc_ref, gc_ref, oc_ref, uec_ref, 0, C), (ul_ref, gl_ref, ol_ref, uel_ref, C, S))

    def tile_rows(row0):
        return pl.ds(pl.multiple_of(row0, SUBLANES), SUBLANES)

    for src_ref, _, _, ue_ref, start, n in segments:
        piece = n // SUBLANES

        def gather(k, _, src_ref=src_ref, ue_ref=ue_ref, piece=piece):
            ue_ref[tile_rows(SUBLANES * (k + 1)), :] = src_ref[pl.ds(k, SUBLANES, stride=piece), :]
            return 0

        lax.fori_loop(0, piece, gather, 0, unroll=8)
        last = ue_ref[SUBLANES * piece:SUBLANES * (piece + 1), :]
        first = ue_ref[SUBLANES:2 * SUBLANES, :]
        second = ue_ref[2 * SUBLANES:3 * SUBLANES, :]
        ue_ref[0:SUBLANES, :] = jnp.where(sub >= 1, pltpu.roll(last, 1, axis=0), 0.0)
        ue_ref[SUBLANES * (piece + 1):SUBLANES * (piece + 2), :] = jnp.where(
            sub < SUBLANES - 1, pltpu.roll(first, SUBLANES - 1, axis=0), 0.0)
        ue_ref[SUBLANES * (piece + 2):SUBLANES * (piece + 3), :] = jnp.where(
            sub < SUBLANES - 1, pltpu.roll(second, SUBLANES - 1, axis=0), 0.0)
        tc = min(LRU_CHUNK, n)
        for c0 in range(0, n, tc):
            taps = [ue_ref[c0 + SUBLANES * j:c0 + SUBLANES * j + tc, :] for j in range(CONV_WIDTH)]
            cv_ref[start + c0:start + c0 + tc, :] = (
                cb + taps[0] * cw[0:1] + taps[1] * cw[1:2] + taps[2] * cw[2:3] + taps[3] * cw[3:4])

    for d in range(2):
        c = (-0.5 * LRU_C * LOG2_E) * _softplus(-lam_ref[d:d + 1, :])
        for c0 in range(0, C + S, LRU_CHUNK):
            uc = cv_ref[c0:c0 + LRU_CHUNK, :]
            ub = uc.astype(BF16)
            half_u = 0.5 * uc
            t_r = jnp.tanh(_dot(ub, wg_ref[2 * d]) + bg_ref[2 * d:2 * d + 1, :])
            t_i = jnp.tanh(_dot(ub, wg_ref[2 * d + 1]) + bg_ref[2 * d + 1:2 * d + 2, :])
            a = jnp.exp2(c * t_r + c)
            a_ref[d, c0:c0 + LRU_CHUNK, :] = a
            y = 1.0 - a * a
            root = jnp.where(y > 0.0, y * lax.rsqrt(y), 0.0)
            b_ref[d, c0:c0 + LRU_CHUNK, :] = root * (half_u * t_i + half_u)

    def scan_segment(start, n):
        piece = n // SUBLANES

        def body(i, carry):
            hf, pf, hb, pb = carry
            rf = tile_rows(start + SUBLANES * i)
            rb = tile_rows(start + SUBLANES * (piece - 1 - i))
            af = a_ref[0, rf, :]
            hf = af * hf + b_ref[0, rf, :]
            pf = af * pf
            hs_ref[0, rf, :] = hf
            ps_ref[0, rf, :] = pf
            ab = a_ref[1, rb, :]
            hb = ab * hb + b_ref[1, rb, :]
            pb = ab * pb
            hs_ref[1, rb, :] = hb
            ps_ref[1, rb, :] = pb
            return hf, pf, hb, pb

        zeros = jnp.zeros((SUBLANES, LANES), F32)
        ones = jnp.ones((SUBLANES, LANES), F32)
        return lax.fori_loop(0, piece, body, (zeros, ones, zeros, ones), unroll=8)

    def piece_inputs(h_fin, p_fin, h0, reverse):
        entering = [None] * SUBLANES
        state = h0
        for r in (range(SUBLANES - 1, -1, -1) if reverse else range(SUBLANES)):
            entering[r] = state
            state = h_fin[r:r + 1, :] + p_fin[r:r + 1, :] * state
        return entering, state

    zero_row = jnp.zeros((1, LANES), F32)
    entering = []
    h0f, h0b = zero_row, zero_row
    for _, _, _, _, start, n in segments:
        hf, pf, hb, pb = scan_segment(start, n)
        ef, h0f = piece_inputs(hf, pf, h0f, reverse=False)
        eb, h0b = piece_inputs(hb, pb, h0b, reverse=True)
        entering.append((ef, eb))

    for (_, g_ref, o_ref, _, start, n), (ef, eb) in zip(segments, entering):
        piece = n // SUBLANES
        enter_f = jnp.concatenate(ef, axis=0)[None]
        enter_b = jnp.concatenate(eb, axis=0)[None]
        tc = min(LRU_CHUNK, n)
        for c0 in range(start, start + n, tc):
            tiles = lambda ref, d: ref[d, c0:c0 + tc, :].reshape(tc // SUBLANES, SUBLANES, LANES)
            total = (tiles(hs_ref, 0) + tiles(ps_ref, 0) * enter_f
                     + tiles(hs_ref, 1) + tiles(ps_ref, 1) * enter_b)
            hs_ref[0, c0:c0 + tc, :] = total.reshape(tc, LANES)
        for r in range(SUBLANES):

            def emit(i, _, g_ref=g_ref, o_ref=o_ref, start=start, piece=piece, r=r):
                halves = []
                for m in (2 * i, 2 * i + 1):
                    rows = pl.ds(start + m * (SUBLANES * SUBLANES) + r, SUBLANES, stride=SUBLANES)
                    halves.append(hs_ref[0, rows, :])
                out_rows = pl.ds(pl.multiple_of(r * piece + 2 * SUBLANES * i, 2 * SUBLANES), 2 * SUBLANES)
                o_ref[out_rows, :] = (g_ref[out_rows, :] * jnp.concatenate(halves, axis=0)).astype(BF16)
                return 0

            lax.fori_loop(0, piece // (2 * SUBLANES), emit, 0, unroll=4)


def _lru_call(layer, u_c, g_c, u_l, g_l, conv_w, conv_b, wg, bg, lam):
    B, C, W = u_c.shape
    S = u_l.shape[1]
    wt = LRU_TILE
    halo = (CONV_WIDTH - 1) * SUBLANES
    seq = lambda n: pl.BlockSpec((None, n, wt), lambda b, j: (b, 0, j))
    return pl.pallas_call(
        _lru_kernel,
        grid=(B, W // wt),
        in_specs=[
            seq(C), seq(C), seq(S), seq(S),
            pl.BlockSpec((None, CONV_WIDTH, wt), lambda b, j: (layer, 0, j)),
            pl.BlockSpec((None, 1, wt), lambda b, j: (layer, 0, j)),
            pl.BlockSpec((None, None, 4, wt, wt), lambda b, j: (layer, j, 0, 0, 0)),
            pl.BlockSpec((None, 4, wt), lambda b, j: (layer, 0, j)),
            pl.BlockSpec((None, 2, wt), lambda b, j: (layer, 0, j)),
        ],
        out_specs=[seq(C), seq(S)],
        out_shape=[jax.ShapeDtypeStruct((B, C, W), BF16), jax.ShapeDtypeStruct((B, S, W), BF16)],
        scratch_shapes=[
            pltpu.VMEM((C + halo, wt), F32),
            pltpu.VMEM((S + halo, wt), F32),
            pltpu.VMEM((C + S, wt), F32),
        ] + [pltpu.VMEM((2, C + S, wt), F32)] * 4,
        compiler_params=_params(2),
        name="rglru",
    )(u_c, g_c, u_l, g_l, conv_w, conv_b, wg, bg, lam)


def _ones_lane(head):
    return V_DIM if head % 2 == 0 else 0


def _attn_kernel(q_ref, *refs):
    o_ref = refs[-1]
    segments = [(refs[2 * i], refs[2 * i + 1]) for i in range((len(refs) - 1) // 2)]
    tq = q_ref.shape[0]

    chunks = []
    for k_ref, v_ref in segments:
        n = k_ref.shape[0]
        kc = min(ATTN_KV_CHUNK, n)
        chunks += [(k_ref, v_ref, slice(c * kc, (c + 1) * kc)) for c in range(n // kc)]

    state = [None] * HEADS_PER_STEP
    for k_ref, v_ref, ks in chunks:
        for hd in range(HEADS_PER_STEP):
            sl = slice(hd * HEAD_PAD, (hd + 1) * HEAD_PAD)
            s = lax.dot_general(q_ref[:, sl], k_ref[ks, sl], (((1,), (1,)), ((), ())),
                                preferred_element_type=F32)
            mc = jnp.max(s, axis=-1, keepdims=True)
            if state[hd] is None:
                m = mc
                acc = _dot(jnp.exp2(s - m).astype(BF16), v_ref[ks, sl])
            else:
                m_old, acc = state[hd]
                m = jnp.maximum(m_old, mc)
                acc = jnp.exp2(m_old - m) * acc + _dot(jnp.exp2(s - m).astype(BF16), v_ref[ks, sl])
            state[hd] = (m, acc)
    outs = []
    for hd, (_, acc) in enumerate(state):
        lane = _ones_lane(hd)
        outs.append(acc / acc[:, lane:lane + 1])
    low = lax.broadcasted_iota(jnp.int32, (tq, LANES), 1) < V_DIM
    for j in range(HEADS_PER_STEP // 2):
        o_ref[:, j * LANES:(j + 1) * LANES] = jnp.where(low, outs[2 * j], outs[2 * j + 1]).astype(BF16)


def _attn_call(q, kv_segments):
    B, Tq, _ = q.shape
    tq = min(ATTN_Q_BLOCK, Tq)
    qw = HEADS_PER_STEP * HEAD_PAD
    vw = HEADS_PER_STEP * V_DIM
    in_specs = [pl.BlockSpec((None, tq, qw), lambda b, p, t: (b, t, p))]
    args = [q]
    for k, v in kv_segments:
        n = k.shape[1]
        in_specs += [pl.BlockSpec((None, n, qw), lambda b, p, t: (b, 0, p))] * 2
        args += [k, v]
    return pl.pallas_call(
        _attn_kernel,
        grid=(B, N_HEADS // HEADS_PER_STEP, Tq // tq),
        in_specs=in_specs,
        out_specs=pl.BlockSpec((None, tq, vw), lambda b, p, t: (b, t, p)),
        out_shape=jax.ShapeDtypeStruct((B, Tq, ATTN_WIDTH), BF16),
        compiler_params=_params(3),
        name="attention",
    )(*args)


def _post_kernel(x_ref, o_ref, m_ref, mod_ref, wo_ref, ng_ref, wfi_ref, wfo_ref, out_ref, *, d_ff):
    x = x_ref[...]
    d_model = x.shape[-1]
    g1 = mod_ref[2:3, :]
    sh2 = mod_ref[3:4, :]
    sc2 = mod_ref[4:5, :]
    g2 = mod_ref[5:6, :]
    y = _dot(o_ref[...], wo_ref[0:ATTN_WIDTH, :]) + _dot(m_ref[...], wo_ref[ATTN_WIDTH:, :])
    x1 = x + g1 * y
    h = x1 * _rms_scale(x1, d_model) * (ng_ref[...] * (1.0 + sc2)) + sh2
    hb = h.astype(BF16)
    acc = jnp.zeros_like(x1)
    for c in range(d_ff // FF_CHUNK):
        lo = c * FF_CHUNK
        gate = _dot(hb, wfi_ref[:, lo:lo + FF_CHUNK])
        up = _dot(hb, wfi_ref[:, d_ff + lo:d_ff + lo + FF_CHUNK])
        act = (gate * jax.nn.sigmoid(gate) * up).astype(BF16)
        acc = acc + _dot(act, wfo_ref[lo:lo + FF_CHUNK, :])
    out_ref[...] = x1 + g2 * acc


def _post_call(layer, x, o, m, mod, mod_row, w_out_b, norm_g, w_ffn_in_b, w_ffn_out_b):
    Bx, Tx, D = x.shape
    tm = ROW_BLOCK
    d_ff = w_ffn_out_b.shape[1]

    def wspec(shape):
        return pl.BlockSpec((None,) + shape, lambda b, t: (layer,) + (0,) * len(shape),
                            pipeline_mode=pl.Buffered(1))

    row = lambda w: pl.BlockSpec((None, tm, w), lambda b, t: (b, t, 0))
    return pl.pallas_call(
        functools.partial(_post_kernel, d_ff=d_ff),
        grid=(Bx, Tx // tm),
        in_specs=[
            row(D),
            row(ATTN_WIDTH),
            row(D - ATTN_WIDTH),
            pl.BlockSpec((None, None, N_MOD, D), lambda b, t: (layer, mod_row(b), 0, 0)),
            wspec((D, D)),
            wspec((1, D)),
            wspec((D, 2 * d_ff)),
            wspec((d_ff, D)),
        ],
        out_specs=row(D),
        out_shape=jax.ShapeDtypeStruct((Bx, Tx, D), F32),
        compiler_params=_params(2),
        name="post_mixer",
    )(x, o, m, mod, w_out_b, norm_g, w_ffn_in_b, w_ffn_out_b)


def _rope_partner():
    r = np.arange(QK_ROPE_DIM)
    i = r % (QK_ROPE_DIM // 2)
    m = QK_ROPE_DIM // 4
    first = i < m
    perm = np.where(first, r + m, r - m)
    sign = np.where(first, -1.0, 1.0).astype(np.float32)
    return perm, sign


def _rope_tables(ang):
    n = ang.shape[0]
    cos_t = jnp.concatenate([jnp.ones((n, QK_NOPE_DIM), F32), jnp.cos(ang),
                             jnp.zeros((n, HEAD_PAD - QK_DIM), F32)], axis=-1)
    sin_t = jnp.concatenate([jnp.zeros((n, QK_NOPE_DIM), F32), jnp.sin(ang),
                             jnp.zeros((n, HEAD_PAD - QK_DIM), F32)], axis=-1)
    return cos_t, sin_t


def _latent_angles(seq):
    m = QK_ROPE_DIM // 4
    inv_freq = ROPE_BASE ** (-jnp.arange(m, dtype=F32) / m)
    pos = jnp.arange(seq, dtype=jnp.int32)
    ang_r = (pos // GRID_W).astype(F32)[:, None] * inv_freq[None, :]
    ang_c = (pos % GRID_W).astype(F32)[:, None] * inv_freq[None, :]
    return jnp.concatenate([ang_r, ang_r, ang_c, ang_c], axis=-1)


def _head_gain(g, perm, scale):
    rope = g[:, QK_NOPE_DIM:]
    return (jnp.concatenate([g, rope[:, perm]], axis=-1) * scale)[:, None, :]


def kernel(x, c, ctx, c_ctx, w_ada, b_ada, norm_mix_g, norm_ffn_g, w_in, q_lora_g, w_uq, kv_lora_g,
           w_ukv, q_norm_g, k_norm_g, conv_w, conv_b, w_rg_a, b_rg_a, w_rg_x, b_rg_x, lru_lambda,
           w_out, w_ffn_in, w_ffn_out):
    B, S, D = x.shape
    C = ctx.shape[1]
    L = w_ada.shape[0]
    d_lru = D - ATTN_WIDTH
    assert S % ROW_BLOCK == 0 and (B * C) % ROW_BLOCK == 0 and S % ATTN_Q_BLOCK == 0
    assert C % LRU_CHUNK == 0 and S % LRU_CHUNK == 0 and S % min(ATTN_KV_CHUNK, S) == 0
    assert d_lru % LRU_TILE == 0 and w_ffn_out.shape[1] % FF_CHUNK == 0
    perm, sign = _rope_partner()

    s0 = Q_LORA_RANK
    s1 = s0 + KV_LORA_RANK
    s2 = s1 + QK_ROPE_DIM
    w_kr = w_in[:, :, s1:s2]
    kr_block = jnp.concatenate([jnp.zeros((L, D, QK_NOPE_DIM), F32), w_kr, w_kr[:, :, perm] * sign], axis=-1)
    w_in_a = jnp.concatenate([w_in[:, :, :s1], kr_block, w_in[:, :, s2:]], axis=-1).astype(BF16)

    wq = w_uq.reshape(L, Q_LORA_RANK, N_HEADS, QK_DIM)
    wq_rope = wq[..., QK_NOPE_DIM:]
    w_uq_a = jnp.concatenate([wq, wq_rope[..., perm] * sign], axis=-1)
    w_uq_a = w_uq_a.reshape(L, Q_LORA_RANK, N_HEADS * HEAD_PAD).astype(BF16)

    wkv = w_ukv.reshape(L, KV_LORA_RANK, N_HEADS, QK_NOPE_DIM + V_DIM)
    wk = jnp.concatenate([wkv[..., :QK_NOPE_DIM],
                          jnp.zeros((L, KV_LORA_RANK, N_HEADS, HEAD_PAD - QK_NOPE_DIM), F32)], axis=-1)
    wv = wkv[..., QK_NOPE_DIM:]
    zv = jnp.zeros_like(wv)
    even = (np.arange(N_HEADS) % 2 == 0)[None, None, :, None]
    wv = jnp.where(even, jnp.concatenate([wv, zv], axis=-1), jnp.concatenate([zv, wv], axis=-1))
    w_ukv_a = jnp.concatenate([wk.reshape(L, KV_LORA_RANK, N_HEADS * HEAD_PAD),
                               wv.reshape(L, KV_LORA_RANK, N_HEADS * HEAD_PAD)], axis=-1).astype(BF16)
    v_ones = np.zeros((1, N_HEADS * HEAD_PAD), np.float32)
    for hd in range(N_HEADS):
        v_ones[0, hd * HEAD_PAD + _ones_lane(hd)] = 1.0
    v_ones = jnp.asarray(v_ones)

    gq = _head_gain(q_norm_g, perm, QK_DIM ** -0.5 * LOG2_E)
    gk = _head_gain(k_norm_g, perm, 1.0)
    cos_l, sin_l = _rope_tables(_latent_angles(S))
    cos_c, sin_c = _rope_tables(jnp.zeros((ROW_BLOCK, QK_ROPE_DIM), F32))

    idx = np.arange(2 * HEAD_PAD)
    same_head = (idx[:, None] // HEAD_PAD) == (idx[None, :] // HEAD_PAD)
    head_ones = jnp.asarray(same_head & ((idx[:, None] % HEAD_PAD) < QK_DIM), BF16)

    per_half = LRU_TILE // (d_lru // LRU_BLOCKS)
    n_half = d_lru // LRU_TILE
    wg = jnp.stack([w_rg_a[:, 0], w_rg_x[:, 0], w_rg_a[:, 1], w_rg_x[:, 1]], axis=1)
    wg = wg.reshape(L, 4, n_half, per_half, d_lru // LRU_BLOCKS, d_lru // LRU_BLOCKS)
    eye = jnp.eye(per_half, dtype=F32)
    wg = jnp.einsum("lghnde,nm->lhgndme", wg, eye)
    wg = (0.5 * wg).reshape(L, n_half, 4, LRU_TILE, LRU_TILE).astype(BF16)
    bg = 0.5 * jnp.stack([b_rg_a[:, 0], b_rg_x[:, 0], b_rg_a[:, 1], b_rg_x[:, 1]], axis=1)

    w_out_b = w_out.astype(BF16)
    w_ffn_in_b = w_ffn_in.astype(BF16)
    w_ffn_out_b = w_ffn_out.astype(BF16)

    pad = (-(B + 1)) % MOD_ROWS_PAD
    s_rows = jnp.concatenate([c, c_ctx[None, :], jnp.zeros((pad, D), F32)], axis=0)
    mod = _modulation(s_rows, w_ada, b_ada[:, None, :])
    mod = mod.reshape(L, B + 1 + pad, N_MOD, D)
    lat_row = lambda b: b
    ctx_row = lambda b: B

    mix_g = norm_mix_g[:, None, :]
    ffn_g = norm_ffn_g[:, None, :]
    qlg = q_lora_g[:, None, :]
    kvlg = kv_lora_g[:, None, :]
    cb = conv_b[:, None, :]
    flat = lambda a: a.reshape(1, B * C, a.shape[-1])
    unflat = lambda a: a.reshape(B, C, a.shape[-1])

    xc = ctx
    for l in range(L):
        last = l == L - 1
        q_l, k_l, v_l, u_l, g_l = _pre_call(l, x, mod, lat_row, True, mix_g, w_in_a, qlg, w_uq_a, kvlg,
                                            w_ukv_a, gq, gk, cos_l, sin_l, head_ones, v_ones)
        q_c, k_c, v_c, u_c, g_c = _pre_call(l, flat(xc), mod, ctx_row, False, mix_g, w_in_a, qlg, w_uq_a,
                                            kvlg, w_ukv_a, gq, gk, cos_c, sin_c, head_ones, v_ones)
        k_c, v_c = unflat(k_c), unflat(v_c)
        m_c, m_l = _lru_call(l, unflat(u_c), unflat(g_c), u_l, g_l, conv_w, cb, wg, bg, lru_lambda)
        o_l = _attn_call(q_l, [(k_c, v_c), (k_l, v_l)])
        x = _post_call(l, x, o_l, m_l, mod, lat_row, w_out_b, ffn_g, w_ffn_in_b, w_ffn_out_b)
        if not last:
            o_c = _attn_call(unflat(q_c), [(k_c, v_c)])
            xc = unflat(_post_call(l, flat(xc), flat(o_c), flat(m_c), mod, ctx_row, w_out_b, ffn_g,
                                   w_ffn_in_b, w_ffn_out_b))
    return x
```

```python
import functools

import jax
import jax.numpy as jnp
import numpy as np
from jax import lax
from jax.experimental import pallas as pl
from jax.experimental.pallas import tpu as pltpu

GRID_W = 64
N_HEADS = 8
QK_NOPE_DIM = 64
QK_ROPE_DIM = 32
QK_DIM = QK_NOPE_DIM + QK_ROPE_DIM
V_DIM = 64
Q_LORA_RANK = 384
KV_LORA_RANK = 256
ATTN_WIDTH = N_HEADS * V_DIM
LRU_BLOCKS = 8
CONV_WIDTH = 4
LRU_C = 8.0
ROPE_BASE = 10000.0
EPS = 1e-6
N_MOD = 6
LOG2_E = 1.4426950408889634

LANES = 128
SUBLANES = 8
HEAD_PAD = LANES
HEADS_PER_STEP = 8
ROW_BLOCK = 1024
ATTN_Q_BLOCK = 1024
ATTN_KV_CHUNK = 2048
LRU_CHUNK = 256
LRU_TILE = LANES
FF_CHUNK = 256
MOD_ROWS_PAD = 8
VMEM_LIMIT = 56 * 1024 * 1024

F32 = jnp.float32
BF16 = jnp.bfloat16


def _params(n_axes, vmem=VMEM_LIMIT):
    return pltpu.CompilerParams(dimension_semantics=("arbitrary",) * n_axes, vmem_limit_bytes=vmem)


def _dot(a, b):
    return jnp.dot(a, b, preferred_element_type=F32)


def _rms_scale(x, n):
    return lax.rsqrt(jnp.sum(x * x, axis=-1, keepdims=True) * (1.0 / n) + EPS)


def _mod_kernel(s_ref, w_ref, b_ref, o_ref):
    s = s_ref[...]
    s = s * jax.nn.sigmoid(s)
    o_ref[...] = _dot(s.astype(BF16), w_ref[...].astype(BF16)) + b_ref[...]


def _modulation(s_rows, w_ada, b_ada):
    L, D, N = w_ada.shape
    R = s_rows.shape[0]
    tn = N // N_MOD
    return pl.pallas_call(
        _mod_kernel,
        grid=(L, N // tn),
        in_specs=[
            pl.BlockSpec((R, D), lambda l, j: (0, 0)),
            pl.BlockSpec((None, D, tn), lambda l, j: (l, 0, j)),
            pl.BlockSpec((None, 1, tn), lambda l, j: (l, 0, j)),
        ],
        out_specs=pl.BlockSpec((None, R, tn), lambda l, j: (l, 0, j)),
        out_shape=jax.ShapeDtypeStruct((L, R, N), F32),
        compiler_params=_params(2),
        name="adaln_mod",
    )(s_rows, w_ada, b_ada)


def _pre_kernel(x_ref, mod_ref, ng_ref, win_ref, qlg_ref, wuq_ref, kvlg_ref, wukv_ref, gq_ref,
                gk_ref, cos_ref, sin_ref, ones_ref, vone_ref, q_ref, k_ref, v_ref, u_ref, g_ref, *, d_lru):
    x = x_ref[...]
    d_model = x.shape[-1]
    sh1 = mod_ref[0:1, :]
    sc1 = mod_ref[1:2, :]
    h = x * _rms_scale(x, d_model) * (ng_ref[...] * (1.0 + sc1)) + sh1
    hb = h.astype(BF16)

    c0 = Q_LORA_RANK
    c1 = c0 + KV_LORA_RANK
    c2 = c1 + HEAD_PAD
    c3 = c2 + d_lru
    c4 = c3 + d_lru
    cq = _dot(hb, win_ref[:, 0:c0])
    ckv = _dot(hb, win_ref[:, c0:c1])
    kr = _dot(hb, win_ref[:, c1:c2])
    u_ref[...] = _dot(hb, win_ref[:, c2:c3])
    g_ref[...] = jax.nn.gelu(_dot(hb, win_ref[:, c3:c4]))

    cqn = (cq * _rms_scale(cq, Q_LORA_RANK) * qlg_ref[...]).astype(BF16)
    ckvn = (ckv * _rms_scale(ckv, KV_LORA_RANK) * kvlg_ref[...]).astype(BF16)
    q_raw = _dot(cqn, wuq_ref[...])
    k_raw = _dot(ckvn, wukv_ref[:, 0:N_HEADS * HEAD_PAD])
    v_ref[...] = (_dot(ckvn, wukv_ref[:, N_HEADS * HEAD_PAD:]) + vone_ref[...]).astype(BF16)

    cos = cos_ref[...]
    sin = sin_ref[...]
    ones = ones_ref[...]

    def pair_norm_rope(raw, gain, dst_ref, j):
        sq = raw * raw
        hi = sq.astype(BF16)
        lo = (sq - hi.astype(F32)).astype(BF16)
        ss = _dot(hi, ones) + _dot(lo, ones)
        xn = raw * lax.rsqrt(ss * (1.0 / QK_DIM) + EPS)
        for i in range(2):
            xh = xn[:, i * HEAD_PAD:(i + 1) * HEAD_PAD] * gain
            out = xh * cos + pltpu.roll(xh, HEAD_PAD - QK_ROPE_DIM, axis=1) * sin
            hd = 2 * j + i
            dst_ref[:, hd * HEAD_PAD:(hd + 1) * HEAD_PAD] = out.astype(BF16)

    kr2 = jnp.concatenate([kr, kr], axis=1)
    for j in range(N_HEADS // 2):
        sl = slice(2 * j * HEAD_PAD, 2 * (j + 1) * HEAD_PAD)
        pair_norm_rope(q_raw[:, sl], gq_ref[...], q_ref, j)
        pair_norm_rope(k_raw[:, sl] + kr2, gk_ref[...], k_ref, j)


def _pre_call(layer, x, mod, mod_row, tables_by_block, norm_g, w_in_a, q_lora_g, w_uq_a, kv_lora_g,
              w_ukv_a, gq, gk, cos_t, sin_t, head_ones, v_ones):
    Bx, Tx, D = x.shape
    tm = ROW_BLOCK
    d_lru = D - ATTN_WIDTH
    n_in = w_in_a.shape[-1]
    hp = N_HEADS * HEAD_PAD

    def wspec(shape):
        return pl.BlockSpec((None,) + shape, lambda b, t: (layer,) + (0,) * len(shape))

    row = lambda w: pl.BlockSpec((None, tm, w), lambda b, t: (b, t, 0))
    table = pl.BlockSpec((tm, HEAD_PAD), (lambda b, t: (t, 0)) if tables_by_block else (lambda b, t: (0, 0)))
    return pl.pallas_call(
        functools.partial(_pre_kernel, d_lru=d_lru),
        grid=(Bx, Tx // tm),
        in_specs=[
            row(D),
            pl.BlockSpec((None, None, N_MOD, D), lambda b, t: (layer, mod_row(b), 0, 0)),
            wspec((1, D)),
            wspec((D, n_in)),
            wspec((1, Q_LORA_RANK)),
            wspec((Q_LORA_RANK, hp)),
            wspec((1, KV_LORA_RANK)),
            wspec((KV_LORA_RANK, 2 * hp)),
            wspec((1, HEAD_PAD)),
            wspec((1, HEAD_PAD)),
            table,
            table,
            pl.BlockSpec((2 * HEAD_PAD, 2 * HEAD_PAD), lambda b, t: (0, 0)),
            pl.BlockSpec((1, hp), lambda b, t: (0, 0)),
        ],
        out_specs=[row(hp), row(hp), row(hp), row(d_lru), row(d_lru)],
        out_shape=[
            jax.ShapeDtypeStruct((Bx, Tx, hp), BF16),
            jax.ShapeDtypeStruct((Bx, Tx, hp), BF16),
            jax.ShapeDtypeStruct((Bx, Tx, hp), BF16),
            jax.ShapeDtypeStruct((Bx, Tx, d_lru), F32),
            jax.ShapeDtypeStruct((Bx, Tx, d_lru), F32),
        ],
        compiler_params=_params(2),
        name="pre_mixer",
    )(x, mod, norm_g, w_in_a, q_lora_g, w_uq_a, kv_lora_g, w_ukv_a, gq, gk, cos_t, sin_t, head_ones, v_ones)


def _softplus(x):
    return jnp.maximum(x, 0.0) + jnp.log1p(jnp.exp(-jnp.abs(x)))


def _lru_kernel(uc_ref, gc_ref, ul_ref, gl_ref, cw_ref, cb_ref, wg_ref, bg_ref, lam_ref,
                oc_ref, ol_ref, uec_ref, uel_ref, cv_ref, a_ref, b_ref, hs_ref, ps_ref):
    C = uc_ref.shape[0]
    S = ul_ref.shape[0]
    cw = cw_ref[...]
    cb = cb_ref[...]
    sub = lax.broadcasted_iota(jnp.int32, (SUBLANES, LANES), 0)
    segments = ((uc_ref, gc_ref, oc_ref, uec_ref, 0, C), (ul_ref, gl_ref, ol_ref, uel_ref, C, S))

    def tile_rows(row0):
        return pl.ds(pl.multiple_of(row0, SUBLANES), SUBLANES)

    for src_ref, _, _, ue_ref, start, n in segments:
        piece = n // SUBLANES

        def gather(k, _, src_ref=src_ref, ue_ref=ue_ref, piece=piece):
            ue_ref[tile_rows(SUBLANES * (k + 1)), :] = src_ref[pl.ds(k, SUBLANES, stride=piece), :]
            return 0

        lax.fori_loop(0, piece, gather, 0, unroll=8)
        last = ue_ref[SUBLANES * piece:SUBLANES * (piece + 1), :]
        first = ue_ref[SUBLANES:2 * SUBLANES, :]
        second = ue_ref[2 * SUBLANES:3 * SUBLANES, :]
        ue_ref[0:SUBLANES, :] = jnp.where(sub >= 1, pltpu.roll(last, 1, axis=0), 0.0)
        ue_ref[SUBLANES * (piece + 1):SUBLANES * (piece + 2), :] = jnp.where(
            sub < SUBLANES - 1, pltpu.roll(first, SUBLANES - 1, axis=0), 0.0)
        ue_ref[SUBLANES * (piece + 2):SUBLANES * (piece + 3), :] = jnp.where(
            sub < SUBLANES - 1, pltpu.roll(second, SUBLANES - 1, axis=0), 0.0)
        tc = min(LRU_CHUNK, n)
        for c0 in range(0, n, tc):
            taps = [ue_ref[c0 + SUBLANES * j:c0 + SUBLANES * j + tc, :] for j in range(CONV_WIDTH)]
            cv_ref[start + c0:start + c0 + tc, :] = (
                cb + taps[0] * cw[0:1] + taps[1] * cw[1:2] + taps[2] * cw[2:3] + taps[3] * cw[3:4])

    for d in range(2):
        c = (-0.5 * LRU_C * LOG2_E) * _softplus(-lam_ref[d:d + 1, :])
        for c0 in range(0, C + S, LRU_CHUNK):
            uc = cv_ref[c0:c0 + LRU_CHUNK, :]
            ub = uc.astype(BF16)
            half_u = 0.5 * uc
            t_r = jnp.tanh(_dot(ub, wg_ref[2 * d]) + bg_ref[2 * d:2 * d + 1, :])
            t_i = jnp.tanh(_dot(ub, wg_ref[2 * d + 1]) + bg_ref[2 * d + 1:2 * d + 2, :])
            a = jnp.exp2(c * t_r + c)
            a_ref[d, c0:c0 + LRU_CHUNK, :] = a
            y = 1.0 - a * a
            root = jnp.where(y > 0.0, y * lax.rsqrt(y), 0.0)
            b_ref[d, c0:c0 + LRU_CHUNK, :] = root * (half_u * t_i + half_u)

    def scan_segment(start, n):
        piece = n // SUBLANES

        def body(i, carry):
            hf, pf, hb, pb = carry
            rf = tile_rows(start + SUBLANES * i)
            rb = tile_rows(start + SUBLANES * (piece - 1 - i))
            af = a_ref[0, rf, :]
            hf = af * hf + b_ref[0, rf, :]
            pf = af * pf
            hs_ref[0, rf, :] = hf
            ps_ref[0, rf, :] = pf
            ab = a_ref[1, rb, :]
            hb = ab * hb + b_ref[1, rb, :]
            pb = ab * pb
            hs_ref[1, rb, :] = hb
            ps_ref[1, rb, :] = pb
            return hf, pf, hb, pb

        zeros = jnp.zeros((SUBLANES, LANES), F32)
        ones = jnp.ones((SUBLANES, LANES), F32)
        return lax.fori_loop(0, piece, body, (zeros, ones, zeros, ones), unroll=8)

    def piece_inputs(h_fin, p_fin, h0, reverse):
        entering = [None] * SUBLANES
        state = h0
        for r in (range(SUBLANES - 1, -1, -1) if reverse else range(SUBLANES)):
            entering[r] = state
            state = h_fin[r:r + 1, :] + p_fin[r:r + 1, :] * state
        return entering, state

    zero_row = jnp.zeros((1, LANES), F32)
    entering = []
    h0f, h0b = zero_row, zero_row
    for _, _, _, _, start, n in segments:
        hf, pf, hb, pb = scan_segment(start, n)
        ef, h0f = piece_inputs(hf, pf, h0f, reverse=False)
        eb, h0b = piece_inputs(hb, pb, h0b, reverse=True)
        entering.append((ef, eb))

    for (_, g_ref, o_ref, _, start, n), (ef, eb) in zip(segments, entering):
        piece = n // SUBLANES
        enter_f = jnp.concatenate(ef, axis=0)[None]
        enter_b = jnp.concatenate(eb, axis=0)[None]
        tc = min(LRU_CHUNK, n)
        for c0 in range(start, start + n, tc):
            tiles = lambda ref, d: ref[d, c0:c0 + tc, :].reshape(tc // SUBLANES, SUBLANES, LANES)
            total = (tiles(hs_ref, 0) + tiles(ps_ref, 0) * enter_f
                     + tiles(hs_ref, 1) + tiles(ps_ref, 1) * enter_b)
            hs_ref[0, c0:c0 + tc, :] = total.reshape(tc, LANES)
        for r in range(SUBLANES):

            def emit(i, _, g_ref=g_ref, o_ref=o_ref, start=start, piece=piece, r=r):
                halves = []
                for m in (2 * i, 2 * i + 1):
                    rows = pl.ds(start + m * (SUBLANES * SUBLANES) + r, SUBLANES, stride=SUBLANES)
                    halves.append(hs_ref[0, rows, :])
                out_rows = pl.ds(pl.multiple_of(r * piece + 2 * SUBLANES * i, 2 * SUBLANES), 2 * SUBLANES)
                o_ref[out_rows, :] = (g_ref[out_rows, :] * jnp.concatenate(halves, axis=0)).astype(BF16)
                return 0

            lax.fori_loop(0, piece // (2 * SUBLANES), emit, 0, unroll=4)


def _lru_call(layer, u_c, g_c, u_l, g_l, conv_w, conv_b, wg, bg, lam):
    B, C, W = u_c.shape
    S = u_l.shape[1]
    wt = LRU_TILE
    halo = (CONV_WIDTH - 1) * SUBLANES
    seq = lambda n: pl.BlockSpec((None, n, wt), lambda b, j: (b, 0, j))
    return pl.pallas_call(
        _lru_kernel,
        grid=(B, W // wt),
        in_specs=[
            seq(C), seq(C), seq(S), seq(S),
            pl.BlockSpec((None, CONV_WIDTH, wt), lambda b, j: (layer, 0, j)),
            pl.BlockSpec((None, 1, wt), lambda b, j: (layer, 0, j)),
            pl.BlockSpec((None, None, 4, wt, wt), lambda b, j: (layer, j, 0, 0, 0)),
            pl.BlockSpec((None, 4, wt), lambda b, j: (layer, 0, j)),
            pl.BlockSpec((None, 2, wt), lambda b, j: (layer, 0, j)),
        ],
        out_specs=[seq(C), seq(S)],
        out_shape=[jax.ShapeDtypeStruct((B, C, W), BF16), jax.ShapeDtypeStruct((B, S, W), BF16)],
        scratch_shapes=[
            pltpu.VMEM((C + halo, wt), F32),
            pltpu.VMEM((S + halo, wt), F32),
            pltpu.VMEM((C + S, wt), F32),
        ] + [pltpu.VMEM((2, C + S, wt), F32)] * 4,
        compiler_params=_params(2),
        name="rglru",
    )(u_c, g_c, u_l, g_l, conv_w, conv_b, wg, bg, lam)


def _ones_lane(head):
    return V_DIM if head % 2 == 0 else 0


def _attn_kernel(q_ref, *refs):
    o_ref = refs[-1]
    segments = [(refs[2 * i], refs[2 * i + 1]) for i in range((len(refs) - 1) // 2)]
    tq = q_ref.shape[0]

    chunks = []
    for k_ref, v_ref in segments:
        n = k_ref.shape[0]
        kc = min(ATTN_KV_CHUNK, n)
        chunks += [(k_ref, v_ref, slice(c * kc, (c + 1) * kc)) for c in range(n // kc)]

    state = [None] * HEADS_PER_STEP
    for k_ref, v_ref, ks in chunks:
        for hd in range(HEADS_PER_STEP):
            sl = slice(hd * HEAD_PAD, (hd + 1) * HEAD_PAD)
            s = lax.dot_general(q_ref[:, sl], k_ref[ks, sl], (((1,), (1,)), ((), ())),
                                preferred_element_type=F32)
            mc = jnp.max(s, axis=-1, keepdims=True)
            if state[hd] is None:
                m = mc
                acc = _dot(jnp.exp2(s - m).astype(BF16), v_ref[ks, sl])
            else:
                m_old, acc = state[hd]
                m = jnp.maximum(m_old, mc)
                acc = jnp.exp2(m_old - m) * acc + _dot(jnp.exp2(s - m).astype(BF16), v_ref[ks, sl])
            state[hd] = (m, acc)
    outs = []
    for hd, (_, acc) in enumerate(state):
        lane = _ones_lane(hd)
        outs.append(acc / acc[:, lane:lane + 1])
    low = lax.broadcasted_iota(jnp.int32, (tq, LANES), 1) < V_DIM
    for j in range(HEADS_PER_STEP // 2):
        o_ref[:, j * LANES:(j + 1) * LANES] = jnp.where(low, outs[2 * j], outs[2 * j + 1]).astype(BF16)


def _attn_call(q, kv_segments):
    B, Tq, _ = q.shape
    tq = min(ATTN_Q_BLOCK, Tq)
    qw = HEADS_PER_STEP * HEAD_PAD
    vw = HEADS_PER_STEP * V_DIM
    in_specs = [pl.BlockSpec((None, tq, qw), lambda b, p, t: (b, t, p))]
    args = [q]
    for k, v in kv_segments:
        n = k.shape[1]
        in_specs += [pl.BlockSpec((None, n, qw), lambda b, p, t: (b, 0, p))] * 2
        args += [k, v]
    return pl.pallas_call(
        _attn_kernel,
        grid=(B, N_HEADS // HEADS_PER_STEP, Tq // tq),
        in_specs=in_specs,
        out_specs=pl.BlockSpec((None, tq, vw), lambda b, p, t: (b, t, p)),
        out_shape=jax.ShapeDtypeStruct((B, Tq, ATTN_WIDTH), BF16),
        compiler_params=_params(3),
        name="attention",
    )(*args)


def _post_kernel(x_ref, o_ref, m_ref, mod_ref, wo_ref, ng_ref, wfi_ref, wfo_ref, out_ref, *, d_ff):
    x = x_ref[...]
    d_model = x.shape[-1]
    g1 = mod_ref[2:3, :]
    sh2 = mod_ref[3:4, :]
    sc2 = mod_ref[4:5, :]
    g2 = mod_ref[5:6, :]
    y = _dot(o_ref[...], wo_ref[0:ATTN_WIDTH, :]) + _dot(m_ref[...], wo_ref[ATTN_WIDTH:, :])
    x1 = x + g1 * y
    h = x1 * _rms_scale(x1, d_model) * (ng_ref[...] * (1.0 + sc2)) + sh2
    hb = h.astype(BF16)
    acc = jnp.zeros_like(x1)
    for c in range(d_ff // FF_CHUNK):
        lo = c * FF_CHUNK
        gate = _dot(hb, wfi_ref[:, lo:lo + FF_CHUNK])
        up = _dot(hb, wfi_ref[:, d_ff + lo:d_ff + lo + FF_CHUNK])
        act = (gate * jax.nn.sigmoid(gate) * up).astype(BF16)
        acc = acc + _dot(act, wfo_ref[lo:lo + FF_CHUNK, :])
    out_ref[...] = x1 + g2 * acc


def _post_call(layer, x, o, m, mod, mod_row, w_out_b, norm_g, w_ffn_in_b, w_ffn_out_b):
    Bx, Tx, D = x.shape
    tm = ROW_BLOCK
    d_ff = w_ffn_out_b.shape[1]

    def wspec(shape):
        return pl.BlockSpec((None,) + shape, lambda b, t: (layer,) + (0,) * len(shape),
                            pipeline_mode=pl.Buffered(1))

    row = lambda w: pl.BlockSpec((None, tm, w), lambda b, t: (b, t, 0))
    return pl.pallas_call(
        functools.partial(_post_kernel, d_ff=d_ff),
        grid=(Bx, Tx // tm),
        in_specs=[
            row(D),
            row(ATTN_WIDTH),
            row(D - ATTN_WIDTH),
            pl.BlockSpec((None, None, N_MOD, D), lambda b, t: (layer, mod_row(b), 0, 0)),
            wspec((D, D)),
            wspec((1, D)),
            wspec((D, 2 * d_ff)),
            wspec((d_ff, D)),
        ],
        out_specs=row(D),
        out_shape=jax.ShapeDtypeStruct((Bx, Tx, D), F32),
        compiler_params=_params(2),
        name="post_mixer",
    )(x, o, m, mod, w_out_b, norm_g, w_ffn_in_b, w_ffn_out_b)


def _rope_partner():
    r = np.arange(QK_ROPE_DIM)
    i = r % (QK_ROPE_DIM // 2)
    m = QK_ROPE_DIM // 4
    first = i < m
    perm = np.where(first, r + m, r - m)
    sign = np.where(first, -1.0, 1.0).astype(np.float32)
    return perm, sign


def _rope_tables(ang):
    n = ang.shape[0]
    cos_t = jnp.concatenate([jnp.ones((n, QK_NOPE_DIM), F32), jnp.cos(ang),
                             jnp.zeros((n, HEAD_PAD - QK_DIM), F32)], axis=-1)
    sin_t = jnp.concatenate([jnp.zeros((n, QK_NOPE_DIM), F32), jnp.sin(ang),
                             jnp.zeros((n, HEAD_PAD - QK_DIM), F32)], axis=-1)
    return cos_t, sin_t


def _latent_angles(seq):
    m = QK_ROPE_DIM // 4
    inv_freq = ROPE_BASE ** (-jnp.arange(m, dtype=F32) / m)
    pos = jnp.arange(seq, dtype=jnp.int32)
    ang_r = (pos // GRID_W).astype(F32)[:, None] * inv_freq[None, :]
    ang_c = (pos % GRID_W).astype(F32)[:, None] * inv_freq[None, :]
    return jnp.concatenate([ang_r, ang_r, ang_c, ang_c], axis=-1)


def _head_gain(g, perm, scale):
    rope = g[:, QK_NOPE_DIM:]
    return (jnp.concatenate([g, rope[:, perm]], axis=-1) * scale)[:, None, :]


def kernel(x, c, ctx, c_ctx, w_ada, b_ada, norm_mix_g, norm_ffn_g, w_in, q_lora_g, w_uq, kv_lora_g,
           w_ukv, q_norm_g, k_norm_g, conv_w, conv_b, w_rg_a, b_rg_a, w_rg_x, b_rg_x, lru_lambda,
           w_out, w_ffn_in, w_ffn_out):
    B, S, D = x.shape
    C = ctx.shape[1]
    L = w_ada.shape[0]
    d_lru = D - ATTN_WIDTH
    assert S % ROW_BLOCK == 0 and (B * C) % ROW_BLOCK == 0 and S % ATTN_Q_BLOCK == 0
    assert C % LRU_CHUNK == 0 and S % LRU_CHUNK == 0 and S % min(ATTN_KV_CHUNK, S) == 0
    assert d_lru % LRU_TILE == 0 and w_ffn_out.shape[1] % FF_CHUNK == 0
    perm, sign = _rope_partner()

    s0 = Q_LORA_RANK
    s1 = s0 + KV_LORA_RANK
    s2 = s1 + QK_ROPE_DIM
    w_kr = w_in[:, :, s1:s2]
    kr_block = jnp.concatenate([jnp.zeros((L, D, QK_NOPE_DIM), F32), w_kr, w_kr[:, :, perm] * sign], axis=-1)
    w_in_a = jnp.concatenate([w_in[:, :, :s1], kr_block, w_in[:, :, s2:]], axis=-1).astype(BF16)

    wq = w_uq.reshape(L, Q_LORA_RANK, N_HEADS, QK_DIM)
    wq_rope = wq[..., QK_NOPE_DIM:]
    w_uq_a = jnp.concatenate([wq, wq_rope[..., perm] * sign], axis=-1)
    w_uq_a = w_uq_a.reshape(L, Q_LORA_RANK, N_HEADS * HEAD_PAD).astype(BF16)

    wkv = w_ukv.reshape(L, KV_LORA_RANK, N_HEADS, QK_NOPE_DIM + V_DIM)
    wk = jnp.concatenate([wkv[..., :QK_NOPE_DIM],
                          jnp.zeros((L, KV_LORA_RANK, N_HEADS, HEAD_PAD - QK_NOPE_DIM), F32)], axis=-1)
    wv = wkv[..., QK_NOPE_DIM:]
    zv = jnp.zeros_like(wv)
    even = (np.arange(N_HEADS) % 2 == 0)[None, None, :, None]
    wv = jnp.where(even, jnp.concatenate([wv, zv], axis=-1), jnp.concatenate([zv, wv], axis=-1))
    w_ukv_a = jnp.concatenate([wk.reshape(L, KV_LORA_RANK, N_HEADS * HEAD_PAD),
                               wv.reshape(L, KV_LORA_RANK, N_HEADS * HEAD_PAD)], axis=-1).astype(BF16)
    v_ones = np.zeros((1, N_HEADS * HEAD_PAD), np.float32)
    for hd in range(N_HEADS):
        v_ones[0, hd * HEAD_PAD + _ones_lane(hd)] = 1.0
    v_ones = jnp.asarray(v_ones)

    gq = _head_gain(q_norm_g, perm, QK_DIM ** -0.5 * LOG2_E)
    gk = _head_gain(k_norm_g, perm, 1.0)
    cos_l, sin_l = _rope_tables(_latent_angles(S))
    cos_c, sin_c = _rope_tables(jnp.zeros((ROW_BLOCK, QK_ROPE_DIM), F32))

    idx = np.arange(2 * HEAD_PAD)
    same_head = (idx[:, None] // HEAD_PAD) == (idx[None, :] // HEAD_PAD)
    head_ones = jnp.asarray(same_head & ((idx[:, None] % HEAD_PAD) < QK_DIM), BF16)

    per_half = LRU_TILE // (d_lru // LRU_BLOCKS)
    n_half = d_lru // LRU_TILE
    wg = jnp.stack([w_rg_a[:, 0], w_rg_x[:, 0], w_rg_a[:, 1], w_rg_x[:, 1]], axis=1)
    wg = wg.reshape(L, 4, n_half, per_half, d_lru // LRU_BLOCKS, d_lru // LRU_BLOCKS)
    eye = jnp.eye(per_half, dtype=F32)
    wg = jnp.einsum("lghnde,nm->lhgndme", wg, eye)
    wg = (0.5 * wg).reshape(L, n_half, 4, LRU_TILE, LRU_TILE).astype(BF16)
    bg = 0.5 * jnp.stack([b_rg_a[:, 0], b_rg_x[:, 0], b_rg_a[:, 1], b_rg_x[:, 1]], axis=1)

    w_out_b = w_out.astype(BF16)
    w_ffn_in_b = w_ffn_in.astype(BF16)
    w_ffn_out_b = w_ffn_out.astype(BF16)

    pad = (-(B + 1)) % MOD_ROWS_PAD
    s_rows = jnp.concatenate([c, c_ctx[None, :], jnp.zeros((pad, D), F32)], axis=0)
    mod = _modulation(s_rows, w_ada, b_ada[:, None, :])
    mod = mod.reshape(L, B + 1 + pad, N_MOD, D)
    lat_row = lambda b: b
    ctx_row = lambda b: B

    mix_g = norm_mix_g[:, None, :]
    ffn_g = norm_ffn_g[:, None, :]
    qlg = q_lora_g[:, None, :]
    kvlg = kv_lora_g[:, None, :]
    cb = conv_b[:, None, :]
    flat = lambda a: a.reshape(1, B * C, a.shape[-1])
    unflat = lambda a: a.reshape(B, C, a.shape[-1])

    xc = ctx
    for l in range(L):
        last = l == L - 1
        q_l, k_l, v_l, u_l, g_l = _pre_call(l, x, mod, lat_row, True, mix_g, w_in_a, qlg, w_uq_a, kvlg,
                                            w_ukv_a, gq, gk, cos_l, sin_l, head_ones, v_ones)
        q_c, k_c, v_c, u_c, g_c = _pre_call(l, flat(xc), mod, ctx_row, False, mix_g, w_in_a, qlg, w_uq_a,
                                            kvlg, w_ukv_a, gq, gk, cos_c, sin_c, head_ones, v_ones)
        k_c, v_c = unflat(k_c), unflat(v_c)
        m_c, m_l = _lru_call(l, unflat(u_c), unflat(g_c), u_l, g_l, conv_w, cb, wg, bg, lru_lambda)
        o_l = _attn_call(q_l, [(k_c, v_c), (k_l, v_l)])
        x = _post_call(l, x, o_l, m_l, mod, lat_row, w_out_b, ffn_g, w_ffn_in_b, w_ffn_out_b)
        if not last:
            o_c = _attn_call(unflat(q_c), [(k_c, v_c)])
            xc = unflat(_post_call(l, flat(xc), flat(o_c), flat(m_c), mod, ctx_row, w_out_b, ffn_g,
                                   w_ffn_in_b, w_ffn_out_b))
    return x
```

```python
import functools

import jax
import jax.numpy as jnp
import numpy as np
from jax import lax
from jax.experimental import pallas as pl
from jax.experimental.pallas import tpu as pltpu

GRID_W = 64
N_HEADS = 8
QK_NOPE_DIM = 64
QK_ROPE_DIM = 32
QK_DIM = QK_NOPE_DIM + QK_ROPE_DIM
V_DIM = 64
Q_LORA_RANK = 384
KV_LORA_RANK = 256
ATTN_WIDTH = N_HEADS * V_DIM
LRU_BLOCKS = 8
CONV_WIDTH = 4
LRU_C = 8.0
ROPE_BASE = 10000.0
EPS = 1e-6
N_MOD = 6
LOG2_E = 1.4426950408889634

LANES = 128
SUBLANES = 8
HEAD_PAD = LANES
HEADS_PER_STEP = 8
ROW_BLOCK = 1024
ATTN_Q_BLOCK = 1024
ATTN_KV_CHUNK = 2048
LRU_CHUNK = 256
LRU_TILE = LANES
FF_CHUNK = 256
MOD_ROWS_PAD = 8
VMEM_LIMIT = 56 * 1024 * 1024

F32 = jnp.float32
BF16 = jnp.bfloat16


def _params(n_axes, vmem=VMEM_LIMIT):
    return pltpu.CompilerParams(dimension_semantics=("arbitrary",) * n_axes, vmem_limit_bytes=vmem)


def _dot(a, b):
    return jnp.dot(a, b, preferred_element_type=F32)


def _rms_scale(x, n):
    return lax.rsqrt(jnp.sum(x * x, axis=-1, keepdims=True) * (1.0 / n) + EPS)


def _mod_kernel(s_ref, w_ref, b_ref, o_ref):
    s = s_ref[...]
    s = s * jax.nn.sigmoid(s)
    o_ref[...] = _dot(s.astype(BF16), w_ref[...].astype(BF16)) + b_ref[...]


def _modulation(s_rows, w_ada, b_ada):
    L, D, N = w_ada.shape
    R = s_rows.shape[0]
    tn = N // N_MOD
    return pl.pallas_call(
        _mod_kernel,
        grid=(L, N // tn),
        in_specs=[
            pl.BlockSpec((R, D), lambda l, j: (0, 0)),
            pl.BlockSpec((None, D, tn), lambda l, j: (l, 0, j)),
            pl.BlockSpec((None, 1, tn), lambda l, j: (l, 0, j)),
        ],
        out_specs=pl.BlockSpec((None, R, tn), lambda l, j: (l, 0, j)),
        out_shape=jax.ShapeDtypeStruct((L, R, N), F32),
        compiler_params=_params(2),
        name="adaln_mod",
    )(s_rows, w_ada, b_ada)


def _pre_kernel(x_ref, mod_ref, ng_ref, win_ref, qlg_ref, wuq_ref, kvlg_ref, wukv_ref, gq_ref,
                gk_ref, cos_ref, sin_ref, ones_ref, vone_ref, q_ref, k_ref, v_ref, u_ref, g_ref, *, d_lru):
    x = x_ref[...]
    d_model = x.shape[-1]
    sh1 = mod_ref[0:1, :]
    sc1 = mod_ref[1:2, :]
    h = x * _rms_scale(x, d_model) * (ng_ref[...] * (1.0 + sc1)) + sh1
    hb = h.astype(BF16)

    c0 = Q_LORA_RANK
    c1 = c0 + HEAD_PAD
    c2 = c1 + KV_LORA_RANK
    c3 = c2 + d_lru
    c4 = c3 + d_lru
    cq_kr = _dot(hb, win_ref[:, 0:c1])
    cq = cq_kr[:, 0:c0]
    kr = cq_kr[:, c0:c1]
    ckv = _dot(hb, win_ref[:, c1:c2])
    u_ref[...] = _dot(hb, win_ref[:, c2:c3])
    g_ref[...] = jax.nn.gelu(_dot(hb, win_ref[:, c3:c4]))

    cqn = (cq * _rms_scale(cq, Q_LORA_RANK) * qlg_ref[...]).astype(BF16)
    ckvn = (ckv * _rms_scale(ckv, KV_LORA_RANK) * kvlg_ref[...]).astype(BF16)
    q_raw = _dot(cqn, wuq_ref[...])
    k_raw = _dot(ckvn, wukv_ref[:, 0:N_HEADS * HEAD_PAD])
    v_ref[...] = (_dot(ckvn, wukv_ref[:, N_HEADS * HEAD_PAD:]) + vone_ref[...]).astype(BF16)

    cos = cos_ref[...]
    sin = sin_ref[...]
    ones = ones_ref[...]

    def pair_norm_rope(raw, gain, dst_ref, j):
        sq = raw * raw
        hi = sq.astype(BF16)
        lo = (sq - hi.astype(F32)).astype(BF16)
        ss = _dot(hi, ones) + _dot(lo, ones)
        xn = raw * lax.rsqrt(ss * (1.0 / QK_DIM) + EPS)
        for i in range(2):
            xh = xn[:, i * HEAD_PAD:(i + 1) * HEAD_PAD] * gain
            out = xh * cos + pltpu.roll(xh, HEAD_PAD - QK_ROPE_DIM, axis=1) * sin
            hd = 2 * j + i
            dst_ref[:, hd * HEAD_PAD:(hd + 1) * HEAD_PAD] = out.astype(BF16)

    kr2 = jnp.concatenate([kr, kr], axis=1)
    for j in range(N_HEADS // 2):
        sl = slice(2 * j * HEAD_PAD, 2 * (j + 1) * HEAD_PAD)
        pair_norm_rope(q_raw[:, sl], gq_ref[...], q_ref, j)
        pair_norm_rope(k_raw[:, sl] + kr2, gk_ref[...], k_ref, j)


def _pre_call(layer, x, mod, mod_row, tables_by_block, norm_g, w_in_a, q_lora_g, w_uq_a, kv_lora_g,
              w_ukv_a, gq, gk, cos_t, sin_t, head_ones, v_ones):
    Bx, Tx, D = x.shape
    tm = ROW_BLOCK
    d_lru = D - ATTN_WIDTH
    n_in = w_in_a.shape[-1]
    hp = N_HEADS * HEAD_PAD

    def wspec(shape):
        return pl.BlockSpec((None,) + shape, lambda b, t: (layer,) + (0,) * len(shape))

    row = lambda w: pl.BlockSpec((None, tm, w), lambda b, t: (b, t, 0))
    table = pl.BlockSpec((tm, HEAD_PAD), (lambda b, t: (t, 0)) if tables_by_block else (lambda b, t: (0, 0)))
    return pl.pallas_call(
        functools.partial(_pre_kernel, d_lru=d_lru),
        grid=(Bx, Tx // tm),
        in_specs=[
            row(D),
            pl.BlockSpec((None, None, N_MOD, D), lambda b, t: (layer, mod_row(b), 0, 0)),
            wspec((1, D)),
            wspec((D, n_in)),
            wspec((1, Q_LORA_RANK)),
            wspec((Q_LORA_RANK, hp)),
            wspec((1, KV_LORA_RANK)),
            wspec((KV_LORA_RANK, 2 * hp)),
            wspec((1, HEAD_PAD)),
            wspec((1, HEAD_PAD)),
            table,
            table,
            pl.BlockSpec((2 * HEAD_PAD, 2 * HEAD_PAD), lambda b, t: (0, 0)),
            pl.BlockSpec((1, hp), lambda b, t: (0, 0)),
        ],
        out_specs=[row(hp), row(hp), row(hp), row(d_lru), row(d_lru)],
        out_shape=[
            jax.ShapeDtypeStruct((Bx, Tx, hp), BF16),
            jax.ShapeDtypeStruct((Bx, Tx, hp), BF16),
            jax.ShapeDtypeStruct((Bx, Tx, hp), BF16),
            jax.ShapeDtypeStruct((Bx, Tx, d_lru), F32),
            jax.ShapeDtypeStruct((Bx, Tx, d_lru), F32),
        ],
        compiler_params=_params(2),
        name="pre_mixer",
    )(x, mod, norm_g, w_in_a, q_lora_g, w_uq_a, kv_lora_g, w_ukv_a, gq, gk, cos_t, sin_t, head_ones, v_ones)


def _softplus(x):
    return jnp.maximum(x, 0.0) + jnp.log1p(jnp.exp(-jnp.abs(x)))


def _lru_kernel(uc_ref, gc_ref, ul_ref, gl_ref, cw_ref, cb_ref, wg_ref, bg_ref, lam_ref,
                oc_ref, ol_ref, uec_ref, uel_ref, cv_ref, a_ref, b_ref, hs_ref, ps_ref):
    C = uc_ref.shape[0]
    S = ul_ref.shape[0]
    cw = cw_ref[...]
    cb = cb_ref[...]
    sub = lax.broadcasted_iota(jnp.int32, (SUBLANES, LANES), 0)
    segments = ((uc_ref, gc_ref, oc_ref, uec_ref, 0, C), (ul_ref, gl_ref, ol_ref, uel_ref, C, S))

    def tile_rows(row0):
        return pl.ds(row0, SUBLANES)

    def static_loop(n, body, carry):
        for i in range(n):
            carry = body(i, carry)
        return carry

    for src_ref, _, _, ue_ref, start, n in segments:
        piece = n // SUBLANES

        def gather(k, _, src_ref=src_ref, ue_ref=ue_ref, piece=piece):
            ue_ref[tile_rows(SUBLANES * (k + 1)), :] = src_ref[pl.ds(k, SUBLANES, stride=piece), :]
            return 0

        static_loop(piece, gather, 0)
        last = ue_ref[SUBLANES * piece:SUBLANES * (piece + 1), :]
        first = ue_ref[SUBLANES:2 * SUBLANES, :]
        second = ue_ref[2 * SUBLANES:3 * SUBLANES, :]
        ue_ref[0:SUBLANES, :] = jnp.where(sub >= 1, pltpu.roll(last, 1, axis=0), 0.0)
        ue_ref[SUBLANES * (piece + 1):SUBLANES * (piece + 2), :] = jnp.where(
            sub < SUBLANES - 1, pltpu.roll(first, SUBLANES - 1, axis=0), 0.0)
        ue_ref[SUBLANES * (piece + 2):SUBLANES * (piece + 3), :] = jnp.where(
            sub < SUBLANES - 1, pltpu.roll(second, SUBLANES - 1, axis=0), 0.0)
        tc = min(LRU_CHUNK, n)
        for c0 in range(0, n, tc):
            taps = [ue_ref[c0 + SUBLANES * j:c0 + SUBLANES * j + tc, :] for j in range(CONV_WIDTH)]
            cv_ref[start + c0:start + c0 + tc, :] = (
                cb + taps[0] * cw[0:1] + taps[1] * cw[1:2] + taps[2] * cw[2:3] + taps[3] * cw[3:4])

    for d in range(2):
        c = (-0.5 * LRU_C * LOG2_E) * _softplus(-lam_ref[d:d + 1, :])
        for c0 in range(0, C + S, LRU_CHUNK):
            uc = cv_ref[c0:c0 + LRU_CHUNK, :]
            ub = uc.astype(BF16)
            half_u = 0.5 * uc
            t_r = jnp.tanh(_dot(ub, wg_ref[2 * d]) + bg_ref[2 * d:2 * d + 1, :])
            t_i = jnp.tanh(_dot(ub, wg_ref[2 * d + 1]) + bg_ref[2 * d + 1:2 * d + 2, :])
            a = jnp.exp2(c * t_r + c)
            a_ref[d, c0:c0 + LRU_CHUNK, :] = a
            y = 1.0 - a * a
            root = jnp.where(y > 0.0, y * lax.rsqrt(y), 0.0)
            b_ref[d, c0:c0 + LRU_CHUNK, :] = root * (half_u * t_i + half_u)

    def scan_segment(start, n):
        piece = n // SUBLANES

        def body(i, carry):
            hf, pf, hb, pb = carry
            rf = tile_rows(start + SUBLANES * i)
            rb = tile_rows(start + SUBLANES * (piece - 1 - i))
            af = a_ref[0, rf, :]
            hf = af * hf + b_ref[0, rf, :]
            pf = af * pf
            hs_ref[0, rf, :] = hf
            ps_ref[0, rf, :] = pf
            ab = a_ref[1, rb, :]
            hb = ab * hb + b_ref[1, rb, :]
            pb = ab * pb
            hs_ref[1, rb, :] = hb
            ps_ref[1, rb, :] = pb
            return hf, pf, hb, pb

        zeros = jnp.zeros((SUBLANES, LANES), F32)
        ones = jnp.ones((SUBLANES, LANES), F32)
        return static_loop(piece, body, (zeros, ones, zeros, ones))

    def piece_inputs(h_fin, p_fin, h0, reverse):
        entering = [None] * SUBLANES
        state = h0
        for r in (range(SUBLANES - 1, -1, -1) if reverse else range(SUBLANES)):
            entering[r] = state
            state = h_fin[r:r + 1, :] + p_fin[r:r + 1, :] * state
        return entering, state

    zero_row = jnp.zeros((1, LANES), F32)
    entering = []
    h0f, h0b = zero_row, zero_row
    for _, _, _, _, start, n in segments:
        hf, pf, hb, pb = scan_segment(start, n)
        ef, h0f = piece_inputs(hf, pf, h0f, reverse=False)
        eb, h0b = piece_inputs(hb, pb, h0b, reverse=True)
        entering.append((ef, eb))

    for (_, g_ref, o_ref, _, start, n), (ef, eb) in zip(segments, entering):
        piece = n // SUBLANES
        enter_f = jnp.concatenate(ef, axis=0)[None]
        enter_b = jnp.concatenate(eb, axis=0)[None]
        tc = min(LRU_CHUNK, n)
        for c0 in range(start, start + n, tc):
            tiles = lambda ref, d: ref[d, c0:c0 + tc, :].reshape(tc // SUBLANES, SUBLANES, LANES)
            total = (tiles(hs_ref, 0) + tiles(ps_ref, 0) * enter_f
                     + tiles(hs_ref, 1) + tiles(ps_ref, 1) * enter_b)
            hs_ref[0, c0:c0 + tc, :] = total.reshape(tc, LANES)
        for r in range(SUBLANES):

            def emit(i, _, g_ref=g_ref, o_ref=o_ref, start=start, piece=piece, r=r):
                halves = []
                for m in (2 * i, 2 * i + 1):
                    rows = pl.ds(start + m * (SUBLANES * SUBLANES) + r, SUBLANES, stride=SUBLANES)
                    halves.append(hs_ref[0, rows, :])
                out_rows = pl.ds(r * piece + 2 * SUBLANES * i, 2 * SUBLANES)
                o_ref[out_rows, :] = (g_ref[out_rows, :] * jnp.concatenate(halves, axis=0)).astype(BF16)
                return 0

            static_loop(piece // (2 * SUBLANES), emit, 0)


def _lru_call(layer, u_c, g_c, u_l, g_l, conv_w, conv_b, wg, bg, lam):
    B, C, W = u_c.shape
    S = u_l.shape[1]
    wt = LRU_TILE
    halo = (CONV_WIDTH - 1) * SUBLANES
    seq = lambda n: pl.BlockSpec((None, n, wt), lambda b, j: (b, 0, j))
    return pl.pallas_call(
        _lru_kernel,
        grid=(B, W // wt),
        in_specs=[
            seq(C), seq(C), seq(S), seq(S),
            pl.BlockSpec((None, CONV_WIDTH, wt), lambda b, j: (layer, 0, j)),
            pl.BlockSpec((None, 1, wt), lambda b, j: (layer, 0, j)),
            pl.BlockSpec((None, None, 4, wt, wt), lambda b, j: (layer, j, 0, 0, 0)),
            pl.BlockSpec((None, 4, wt), lambda b, j: (layer, 0, j)),
            pl.BlockSpec((None, 2, wt), lambda b, j: (layer, 0, j)),
        ],
        out_specs=[seq(C), seq(S)],
        out_shape=[jax.ShapeDtypeStruct((B, C, W), BF16), jax.ShapeDtypeStruct((B, S, W), BF16)],
        scratch_shapes=[
            pltpu.VMEM((C + halo, wt), F32),
            pltpu.VMEM((S + halo, wt), F32),
            pltpu.VMEM((C + S, wt), F32),
        ] + [pltpu.VMEM((2, C + S, wt), F32)] * 4,
        compiler_params=_params(2),
        name="rglru",
    )(u_c, g_c, u_l, g_l, conv_w, conv_b, wg, bg, lam)


def _ones_lane(head):
    return V_DIM if head % 2 == 0 else 0


def _attn_kernel(q_ref, *refs):
    o_ref = refs[-1]
    segments = [(refs[2 * i], refs[2 * i + 1]) for i in range((len(refs) - 1) // 2)]
    tq = q_ref.shape[0]

    chunks = []
    for k_ref, v_ref in segments:
        n = k_ref.shape[0]
        kc = min(ATTN_KV_CHUNK, n)
        chunks += [(k_ref, v_ref, slice(c * kc, (c + 1) * kc)) for c in range(n // kc)]

    state = [None] * HEADS_PER_STEP
    for k_ref, v_ref, ks in chunks:
        for hd in range(HEADS_PER_STEP):
            sl = slice(hd * HEAD_PAD, (hd + 1) * HEAD_PAD)
            s = lax.dot_general(q_ref[:, sl], k_ref[ks, sl], (((1,), (1,)), ((), ())),
                                preferred_element_type=F32)
            mc = jnp.max(s, axis=-1, keepdims=True)
            if state[hd] is None:
                m = mc
                acc = _dot(jnp.exp2(s - m).astype(BF16), v_ref[ks, sl])
            else:
                m_old, acc = state[hd]
                m = jnp.maximum(m_old, mc)
                acc = jnp.exp2(m_old - m) * acc + _dot(jnp.exp2(s - m).astype(BF16), v_ref[ks, sl])
            state[hd] = (m, acc)
    outs = []
    for hd, (_, acc) in enumerate(state):
        lane = _ones_lane(hd)
        outs.append(acc / acc[:, lane:lane + 1])
    low = lax.broadcasted_iota(jnp.int32, (tq, LANES), 1) < V_DIM
    for j in range(HEADS_PER_STEP // 2):
        o_ref[:, j * LANES:(j + 1) * LANES] = jnp.where(low, outs[2 * j], outs[2 * j + 1]).astype(BF16)


def _attn_call(q, kv_segments):
    B, Tq, _ = q.shape
    tq = min(ATTN_Q_BLOCK, Tq)
    qw = HEADS_PER_STEP * HEAD_PAD
    vw = HEADS_PER_STEP * V_DIM
    in_specs = [pl.BlockSpec((None, tq, qw), lambda b, p, t: (b, t, p))]
    args = [q]
    for k, v in kv_segments:
        n = k.shape[1]
        in_specs += [pl.BlockSpec((None, n, qw), lambda b, p, t: (b, 0, p))] * 2
        args += [k, v]
    return pl.pallas_call(
        _attn_kernel,
        grid=(B, N_HEADS // HEADS_PER_STEP, Tq // tq),
        in_specs=in_specs,
        out_specs=pl.BlockSpec((None, tq, vw), lambda b, p, t: (b, t, p)),
        out_shape=jax.ShapeDtypeStruct((B, Tq, ATTN_WIDTH), BF16),
        compiler_params=_params(3),
        name="attention",
    )(*args)


def _post_kernel(x_ref, o_ref, m_ref, mod_ref, wo_ref, ng_ref, wfi_ref, wfo_ref, out_ref, *, d_ff):
    x = x_ref[...]
    d_model = x.shape[-1]
    g1 = mod_ref[2:3, :]
    sh2 = mod_ref[3:4, :]
    sc2 = mod_ref[4:5, :]
    g2 = mod_ref[5:6, :]
    y = _dot(o_ref[...], wo_ref[0:ATTN_WIDTH, :]) + _dot(m_ref[...], wo_ref[ATTN_WIDTH:, :])
    x1 = x + g1 * y
    h = x1 * _rms_scale(x1, d_model) * (ng_ref[...] * (1.0 + sc2)) + sh2
    hb = h.astype(BF16)
    acc = jnp.zeros_like(x1)
    for c in range(d_ff // FF_CHUNK):
        lo = c * FF_CHUNK
        gate = _dot(hb, wfi_ref[:, lo:lo + FF_CHUNK])
        up = _dot(hb, wfi_ref[:, d_ff + lo:d_ff + lo + FF_CHUNK])
        act = (gate * jax.nn.sigmoid(gate) * up).astype(BF16)
        acc = acc + _dot(act, wfo_ref[lo:lo + FF_CHUNK, :])
    out_ref[...] = x1 + g2 * acc


def _post_call(layer, x, o, m, mod, mod_row, w_out_b, norm_g, w_ffn_in_b, w_ffn_out_b):
    Bx, Tx, D = x.shape
    tm = ROW_BLOCK
    d_ff = w_ffn_out_b.shape[1]

    def wspec(shape):
        return pl.BlockSpec((None,) + shape, lambda b, t: (layer,) + (0,) * len(shape),
                            pipeline_mode=pl.Buffered(1))

    row = lambda w: pl.BlockSpec((None, tm, w), lambda b, t: (b, t, 0))
    return pl.pallas_call(
        functools.partial(_post_kernel, d_ff=d_ff),
        grid=(Bx, Tx // tm),
        in_specs=[
            row(D),
            row(ATTN_WIDTH),
            row(D - ATTN_WIDTH),
            pl.BlockSpec((None, None, N_MOD, D), lambda b, t: (layer, mod_row(b), 0, 0)),
            wspec((D, D)),
            wspec((1, D)),
            wspec((D, 2 * d_ff)),
            wspec((d_ff, D)),
        ],
        out_specs=row(D),
        out_shape=jax.ShapeDtypeStruct((Bx, Tx, D), F32),
        compiler_params=_params(2),
        name="post_mixer",
    )(x, o, m, mod, w_out_b, norm_g, w_ffn_in_b, w_ffn_out_b)


def _rope_partner():
    r = np.arange(QK_ROPE_DIM)
    i = r % (QK_ROPE_DIM // 2)
    m = QK_ROPE_DIM // 4
    first = i < m
    perm = np.where(first, r + m, r - m)
    sign = np.where(first, -1.0, 1.0).astype(np.float32)
    return perm, sign


def _rope_tables(ang):
    n = ang.shape[0]
    cos_t = jnp.concatenate([jnp.ones((n, QK_NOPE_DIM), F32), jnp.cos(ang),
                             jnp.zeros((n, HEAD_PAD - QK_DIM), F32)], axis=-1)
    sin_t = jnp.concatenate([jnp.zeros((n, QK_NOPE_DIM), F32), jnp.sin(ang),
                             jnp.zeros((n, HEAD_PAD - QK_DIM), F32)], axis=-1)
    return cos_t, sin_t


def _latent_angles(seq):
    m = QK_ROPE_DIM // 4
    inv_freq = ROPE_BASE ** (-jnp.arange(m, dtype=F32) / m)
    pos = jnp.arange(seq, dtype=jnp.int32)
    ang_r = (pos // GRID_W).astype(F32)[:, None] * inv_freq[None, :]
    ang_c = (pos % GRID_W).astype(F32)[:, None] * inv_freq[None, :]
    return jnp.concatenate([ang_r, ang_r, ang_c, ang_c], axis=-1)


def _head_gain(g, perm, scale):
    rope = g[:, QK_NOPE_DIM:]
    return (jnp.concatenate([g, rope[:, perm]], axis=-1) * scale)[:, None, :]


def kernel(x, c, ctx, c_ctx, w_ada, b_ada, norm_mix_g, norm_ffn_g, w_in, q_lora_g, w_uq, kv_lora_g,
           w_ukv, q_norm_g, k_norm_g, conv_w, conv_b, w_rg_a, b_rg_a, w_rg_x, b_rg_x, lru_lambda,
           w_out, w_ffn_in, w_ffn_out):
    B, S, D = x.shape
    C = ctx.shape[1]
    L = w_ada.shape[0]
    d_lru = D - ATTN_WIDTH
    assert S % ROW_BLOCK == 0 and (B * C) % ROW_BLOCK == 0 and S % ATTN_Q_BLOCK == 0
    assert C % LRU_CHUNK == 0 and S % LRU_CHUNK == 0 and S % min(ATTN_KV_CHUNK, S) == 0
    assert d_lru % LRU_TILE == 0 and w_ffn_out.shape[1] % FF_CHUNK == 0
    perm, sign = _rope_partner()

    s0 = Q_LORA_RANK
    s1 = s0 + KV_LORA_RANK
    s2 = s1 + QK_ROPE_DIM
    w_kr = w_in[:, :, s1:s2]
    kr_block = jnp.concatenate([jnp.zeros((L, D, QK_NOPE_DIM), F32), w_kr, w_kr[:, :, perm] * sign], axis=-1)
    w_in_a = jnp.concatenate([w_in[:, :, :s0], kr_block, w_in[:, :, s0:s1], w_in[:, :, s2:]],
                             axis=-1).astype(BF16)

    wq = w_uq.reshape(L, Q_LORA_RANK, N_HEADS, QK_DIM)
    wq_rope = wq[..., QK_NOPE_DIM:]
    w_uq_a = jnp.concatenate([wq, wq_rope[..., perm] * sign], axis=-1)
    w_uq_a = w_uq_a.reshape(L, Q_LORA_RANK, N_HEADS * HEAD_PAD).astype(BF16)

    wkv = w_ukv.reshape(L, KV_LORA_RANK, N_HEADS, QK_NOPE_DIM + V_DIM)
    wk = jnp.concatenate([wkv[..., :QK_NOPE_DIM],
                          jnp.zeros((L, KV_LORA_RANK, N_HEADS, HEAD_PAD - QK_NOPE_DIM), F32)], axis=-1)
    wv = wkv[..., QK_NOPE_DIM:]
    zv = jnp.zeros_like(wv)
    even = (np.arange(N_HEADS) % 2 == 0)[None, None, :, None]
    wv = jnp.where(even, jnp.concatenate([wv, zv], axis=-1), jnp.concatenate([zv, wv], axis=-1))
    w_ukv_a = jnp.concatenate([wk.reshape(L, KV_LORA_RANK, N_HEADS * HEAD_PAD),
                               wv.reshape(L, KV_LORA_RANK, N_HEADS * HEAD_PAD)], axis=-1).astype(BF16)
    v_ones = np.zeros((1, N_HEADS * HEAD_PAD), np.float32)
    for hd in range(N_HEADS):
        v_ones[0, hd * HEAD_PAD + _ones_lane(hd)] = 1.0
    v_ones = jnp.asarray(v_ones)

    gq = _head_gain(q_norm_g, perm, QK_DIM ** -0.5 * LOG2_E)
    gk = _head_gain(k_norm_g, perm, 1.0)
    cos_l, sin_l = _rope_tables(_latent_angles(S))
    cos_c, sin_c = _rope_tables(jnp.zeros((ROW_BLOCK, QK_ROPE_DIM), F32))

    idx = np.arange(2 * HEAD_PAD)
    same_head = (idx[:, None] // HEAD_PAD) == (idx[None, :] // HEAD_PAD)
    head_ones = jnp.asarray(same_head & ((idx[:, None] % HEAD_PAD) < QK_DIM), BF16)

    per_half = LRU_TILE // (d_lru // LRU_BLOCKS)
    n_half = d_lru // LRU_TILE
    wg = jnp.stack([w_rg_a[:, 0], w_rg_x[:, 0], w_rg_a[:, 1], w_rg_x[:, 1]], axis=1)
    wg = wg.reshape(L, 4, n_half, per_half, d_lru // LRU_BLOCKS, d_lru // LRU_BLOCKS)
    eye = jnp.eye(per_half, dtype=F32)
    wg = jnp.einsum("lghnde,nm->lhgndme", wg, eye)
    wg = (0.5 * wg).reshape(L, n_half, 4, LRU_TILE, LRU_TILE).astype(BF16)
    bg = 0.5 * jnp.stack([b_rg_a[:, 0], b_rg_x[:, 0], b_rg_a[:, 1], b_rg_x[:, 1]], axis=1)

    w_out_b = w_out.astype(BF16)
    w_ffn_in_b = w_ffn_in.astype(BF16)
    w_ffn_out_b = w_ffn_out.astype(BF16)

    pad = (-(B + 1)) % MOD_ROWS_PAD
    s_rows = jnp.concatenate([c, c_ctx[None, :], jnp.zeros((pad, D), F32)], axis=0)
    mod = _modulation(s_rows, w_ada, b_ada[:, None, :])
    mod = mod.reshape(L, B + 1 + pad, N_MOD, D)
    lat_row = lambda b: b
    ctx_row = lambda b: B

    mix_g = norm_mix_g[:, None, :]
    ffn_g = norm_ffn_g[:, None, :]
    qlg = q_lora_g[:, None, :]
    kvlg = kv_lora_g[:, None, :]
    cb = conv_b[:, None, :]
    flat = lambda a: a.reshape(1, B * C, a.shape[-1])
    unflat = lambda a: a.reshape(B, C, a.shape[-1])

    xc = ctx
    for l in range(L):
        last = l == L - 1
        q_l, k_l, v_l, u_l, g_l = _pre_call(l, x, mod, lat_row, True, mix_g, w_in_a, qlg, w_uq_a, kvlg,
                                            w_ukv_a, gq, gk, cos_l, sin_l, head_ones, v_ones)
        q_c, k_c, v_c, u_c, g_c = _pre_call(l, flat(xc), mod, ctx_row, False, mix_g, w_in_a, qlg, w_uq_a,
                                            kvlg, w_ukv_a, gq, gk, cos_c, sin_c, head_ones, v_ones)
        k_c, v_c = unflat(k_c), unflat(v_c)
        m_c, m_l = _lru_call(l, unflat(u_c), unflat(g_c), u_l, g_l, conv_w, cb, wg, bg, lru_lambda)
        o_l = _attn_call(q_l, [(k_c, v_c), (k_l, v_l)])
        x = _post_call(l, x, o_l, m_l, mod, lat_row, w_out_b, ffn_g, w_ffn_in_b, w_ffn_out_b)
        if not last:
            o_c = _attn_call(unflat(q_c), [(k_c, v_c)])
            xc = unflat(_post_call(l, flat(xc), flat(o_c), flat(m_c), mod, ctx_row, w_out_b, ffn_g,
                                   w_ffn_in_b, w_ffn_out_b))
    return x
```

```python
import functools

import jax
import jax.numpy as jnp
import numpy as np
from jax import lax
from jax.experimental import pallas as pl
from jax.experimental.pallas import tpu as pltpu

GRID_W = 64
N_HEADS = 8
QK_NOPE_DIM = 64
QK_ROPE_DIM = 32
QK_DIM = QK_NOPE_DIM + QK_ROPE_DIM
V_DIM = 64
Q_LORA_RANK = 384
KV_LORA_RANK = 256
ATTN_WIDTH = N_HEADS * V_DIM
LRU_BLOCKS = 8
CONV_WIDTH = 4
LRU_C = 8.0
ROPE_BASE = 10000.0
EPS = 1e-6
N_MOD = 6
LOG2_E = 1.4426950408889634

LANES = 128
SUBLANES = 8
HEAD_PAD = LANES
HEADS_PER_STEP = 8
ROW_BLOCK = 1024
ATTN_Q_BLOCK = 1024
ATTN_KV_CHUNK = 2048
LRU_CHUNK = 256
LRU_TILE = LANES
FF_CHUNK = 256
MOD_ROWS_PAD = 8
VMEM_LIMIT = 56 * 1024 * 1024

F32 = jnp.float32
BF16 = jnp.bfloat16


def _params(n_axes, vmem=VMEM_LIMIT):
    return pltpu.CompilerParams(dimension_semantics=("arbitrary",) * n_axes, vmem_limit_bytes=vmem)


def _dot(a, b):
    return jnp.dot(a, b, preferred_element_type=F32)


def _rms_scale(x, n):
    return lax.rsqrt(jnp.sum(x * x, axis=-1, keepdims=True) * (1.0 / n) + EPS)


def _mod_kernel(s_ref, w_ref, b_ref, o_ref):
    s = s_ref[...]
    s = s * jax.nn.sigmoid(s)
    o_ref[...] = _dot(s.astype(BF16), w_ref[...].astype(BF16)) + b_ref[...]


def _modulation(s_rows, w_ada, b_ada):
    L, D, N = w_ada.shape
    R = s_rows.shape[0]
    tn = N // N_MOD
    return pl.pallas_call(
        _mod_kernel,
        grid=(L, N // tn),
        in_specs=[
            pl.BlockSpec((R, D), lambda l, j: (0, 0)),
            pl.BlockSpec((None, D, tn), lambda l, j: (l, 0, j)),
            pl.BlockSpec((None, 1, tn), lambda l, j: (l, 0, j)),
        ],
        out_specs=pl.BlockSpec((None, R, tn), lambda l, j: (l, 0, j)),
        out_shape=jax.ShapeDtypeStruct((L, R, N), F32),
        compiler_params=_params(2),
        name="adaln_mod",
    )(s_rows, w_ada, b_ada)


def _pre_kernel(x_ref, mod_ref, ng_ref, win_ref, qlg_ref, wuq_ref, kvlg_ref, wukv_ref, gq_ref,
                gk_ref, cos_ref, sin_ref, ones_ref, vone_ref, q_ref, k_ref, v_ref, u_ref, g_ref, *, d_lru):
    x = x_ref[...]
    d_model = x.shape[-1]
    sh1 = mod_ref[0:1, :]
    sc1 = mod_ref[1:2, :]
    h = x * _rms_scale(x, d_model) * (ng_ref[...] * (1.0 + sc1)) + sh1
    hb = h.astype(BF16)

    c0 = Q_LORA_RANK
    c1 = c0 + HEAD_PAD
    c2 = c1 + KV_LORA_RANK
    c3 = c2 + d_lru
    c4 = c3 + d_lru
    cq_kr = _dot(hb, win_ref[:, 0:c1])
    cq = cq_kr[:, 0:c0]
    kr = cq_kr[:, c0:c1]
    ckv = _dot(hb, win_ref[:, c1:c2])
    u_ref[...] = _dot(hb, win_ref[:, c2:c3])
    g_ref[...] = jax.nn.gelu(_dot(hb, win_ref[:, c3:c4]))

    cqn = (cq * _rms_scale(cq, Q_LORA_RANK) * qlg_ref[...]).astype(BF16)
    ckvn = (ckv * _rms_scale(ckv, KV_LORA_RANK) * kvlg_ref[...]).astype(BF16)
    q_raw = _dot(cqn, wuq_ref[...])
    k_raw = _dot(ckvn, wukv_ref[:, 0:N_HEADS * HEAD_PAD])
    v_ref[...] = (_dot(ckvn, wukv_ref[:, N_HEADS * HEAD_PAD:]) + vone_ref[...]).astype(BF16)

    cos = cos_ref[...]
    sin = sin_ref[...]
    ones = ones_ref[...]

    def pair_norm_rope(raw, gain, dst_ref, j):
        ss = _dot((raw * raw).astype(BF16), ones)
        xn = raw * lax.rsqrt(ss * (1.0 / QK_DIM) + EPS)
        for i in range(2):
            xh = xn[:, i * HEAD_PAD:(i + 1) * HEAD_PAD] * gain
            out = xh * cos + pltpu.roll(xh, HEAD_PAD - QK_ROPE_DIM, axis=1) * sin
            hd = 2 * j + i
            dst_ref[:, hd * HEAD_PAD:(hd + 1) * HEAD_PAD] = out.astype(BF16)

    kr2 = jnp.concatenate([kr, kr], axis=1)
    for j in range(N_HEADS // 2):
        sl = slice(2 * j * HEAD_PAD, 2 * (j + 1) * HEAD_PAD)
        pair_norm_rope(q_raw[:, sl], gq_ref[...], q_ref, j)
        pair_norm_rope(k_raw[:, sl] + kr2, gk_ref[...], k_ref, j)


def _pre_call(layer, x, mod, mod_row, tables_by_block, norm_g, w_in_a, q_lora_g, w_uq_a, kv_lora_g,
              w_ukv_a, gq, gk, cos_t, sin_t, head_ones, v_ones):
    Bx, Tx, D = x.shape
    tm = ROW_BLOCK
    d_lru = D - ATTN_WIDTH
    n_in = w_in_a.shape[-1]
    hp = N_HEADS * HEAD_PAD

    def wspec(shape):
        return pl.BlockSpec((None,) + shape, lambda b, t: (layer,) + (0,) * len(shape))

    row = lambda w: pl.BlockSpec((None, tm, w), lambda b, t: (b, t, 0))
    table = pl.BlockSpec((tm, HEAD_PAD), (lambda b, t: (t, 0)) if tables_by_block else (lambda b, t: (0, 0)))
    return pl.pallas_call(
        functools.partial(_pre_kernel, d_lru=d_lru),
        grid=(Bx, Tx // tm),
        in_specs=[
            row(D),
            pl.BlockSpec((None, None, N_MOD, D), lambda b, t: (layer, mod_row(b), 0, 0)),
            wspec((1, D)),
            wspec((D, n_in)),
            wspec((1, Q_LORA_RANK)),
            wspec((Q_LORA_RANK, hp)),
            wspec((1, KV_LORA_RANK)),
            wspec((KV_LORA_RANK, 2 * hp)),
            wspec((1, HEAD_PAD)),
            wspec((1, HEAD_PAD)),
            table,
            table,
            pl.BlockSpec((2 * HEAD_PAD, 2 * HEAD_PAD), lambda b, t: (0, 0)),
            pl.BlockSpec((1, hp), lambda b, t: (0, 0)),
        ],
        out_specs=[row(hp), row(hp), row(hp), row(d_lru), row(d_lru)],
        out_shape=[
            jax.ShapeDtypeStruct((Bx, Tx, hp), BF16),
            jax.ShapeDtypeStruct((Bx, Tx, hp), BF16),
            jax.ShapeDtypeStruct((Bx, Tx, hp), BF16),
            jax.ShapeDtypeStruct((Bx, Tx, d_lru), F32),
            jax.ShapeDtypeStruct((Bx, Tx, d_lru), F32),
        ],
        compiler_params=_params(2),
        name="pre_mixer",
    )(x, mod, norm_g, w_in_a, q_lora_g, w_uq_a, kv_lora_g, w_ukv_a, gq, gk, cos_t, sin_t, head_ones, v_ones)


def _softplus(x):
    return jnp.maximum(x, 0.0) + jnp.log1p(jnp.exp(-jnp.abs(x)))


def _lru_kernel(uc_ref, gc_ref, ul_ref, gl_ref, cw_ref, cb_ref, wg_ref, bg_ref, lam_ref,
                oc_ref, ol_ref, uec_ref, uel_ref, cv_ref, a_ref, b_ref, hs_ref, ps_ref):
    C = uc_ref.shape[0]
    S = ul_ref.shape[0]
    cw = cw_ref[...]
    cb = cb_ref[...]
    sub = lax.broadcasted_iota(jnp.int32, (SUBLANES, LANES), 0)
    segments = ((uc_ref, gc_ref, oc_ref, uec_ref, 0, C), (ul_ref, gl_ref, ol_ref, uel_ref, C, S))

    def tile_rows(row0):
        return pl.ds(row0, SUBLANES)

    def static_loop(n, body, carry):
        for i in range(n):
            carry = body(i, carry)
        return carry

    for src_ref, _, _, ue_ref, start, n in segments:
        piece = n // SUBLANES

        def gather(k, _, src_ref=src_ref, ue_ref=ue_ref, piece=piece):
            ue_ref[tile_rows(SUBLANES * (k + 1)), :] = src_ref[pl.ds(k, SUBLANES, stride=piece), :]
            return 0

        static_loop(piece, gather, 0)
        last = ue_ref[SUBLANES * piece:SUBLANES * (piece + 1), :]
        first = ue_ref[SUBLANES:2 * SUBLANES, :]
        second = ue_ref[2 * SUBLANES:3 * SUBLANES, :]
        ue_ref[0:SUBLANES, :] = jnp.where(sub >= 1, pltpu.roll(last, 1, axis=0), 0.0)
        ue_ref[SUBLANES * (piece + 1):SUBLANES * (piece + 2), :] = jnp.where(
            sub < SUBLANES - 1, pltpu.roll(first, SUBLANES - 1, axis=0), 0.0)
        ue_ref[SUBLANES * (piece + 2):SUBLANES * (piece + 3), :] = jnp.where(
            sub < SUBLANES - 1, pltpu.roll(second, SUBLANES - 1, axis=0), 0.0)
        tc = min(LRU_CHUNK, n)
        for c0 in range(0, n, tc):
            taps = [ue_ref[c0 + SUBLANES * j:c0 + SUBLANES * j + tc, :] for j in range(CONV_WIDTH)]
            cv_ref[start + c0:start + c0 + tc, :] = (
                cb + taps[0] * cw[0:1] + taps[1] * cw[1:2] + taps[2] * cw[2:3] + taps[3] * cw[3:4])

    for d in range(2):
        c = (-0.5 * LRU_C * LOG2_E) * _softplus(-lam_ref[d:d + 1, :])
        for c0 in range(0, C + S, LRU_CHUNK):
            uc = cv_ref[c0:c0 + LRU_CHUNK, :]
            ub = uc.astype(BF16)
            half_u = 0.5 * uc
            t_r = jnp.tanh(_dot(ub, wg_ref[2 * d]) + bg_ref[2 * d:2 * d + 1, :])
            t_i = jnp.tanh(_dot(ub, wg_ref[2 * d + 1]) + bg_ref[2 * d + 1:2 * d + 2, :])
            a = jnp.exp2(c * t_r + c)
            a_ref[d, c0:c0 + LRU_CHUNK, :] = a
            y = 1.0 - a * a
            root = jnp.where(y > 0.0, y * lax.rsqrt(y), 0.0)
            b_ref[d, c0:c0 + LRU_CHUNK, :] = root * (half_u * t_i + half_u)

    def scan_segment(start, n):
        piece = n // SUBLANES

        def body(i, carry):
            hf, pf, hb, pb = carry
            rf = tile_rows(start + SUBLANES * i)
            rb = tile_rows(start + SUBLANES * (piece - 1 - i))
            af = a_ref[0, rf, :]
            hf = af * hf + b_ref[0, rf, :]
            pf = af * pf
            hs_ref[0, rf, :] = hf
            ps_ref[0, rf, :] = pf
            ab = a_ref[1, rb, :]
            hb = ab * hb + b_ref[1, rb, :]
            pb = ab * pb
            hs_ref[1, rb, :] = hb
            ps_ref[1, rb, :] = pb
            return hf, pf, hb, pb

        zeros = jnp.zeros((SUBLANES, LANES), F32)
        ones = jnp.ones((SUBLANES, LANES), F32)
        return static_loop(piece, body, (zeros, ones, zeros, ones))

    def piece_inputs(h_fin, p_fin, h0, reverse):
        entering = [None] * SUBLANES
        state = h0
        for r in (range(SUBLANES - 1, -1, -1) if reverse else range(SUBLANES)):
            entering[r] = state
            state = h_fin[r:r + 1, :] + p_fin[r:r + 1, :] * state
        return entering, state

    zero_row = jnp.zeros((1, LANES), F32)
    entering = []
    h0f, h0b = zero_row, zero_row
    for _, _, _, _, start, n in segments:
        hf, pf, hb, pb = scan_segment(start, n)
        ef, h0f = piece_inputs(hf, pf, h0f, reverse=False)
        eb, h0b = piece_inputs(hb, pb, h0b, reverse=True)
        entering.append((ef, eb))

    for (_, g_ref, o_ref, _, start, n), (ef, eb) in zip(segments, entering):
        piece = n // SUBLANES
        enter_f = jnp.concatenate(ef, axis=0)[None]
        enter_b = jnp.concatenate(eb, axis=0)[None]
        tc = min(LRU_CHUNK, n)
        for c0 in range(start, start + n, tc):
            tiles = lambda ref, d: ref[d, c0:c0 + tc, :].reshape(tc // SUBLANES, SUBLANES, LANES)
            total = (tiles(hs_ref, 0) + tiles(ps_ref, 0) * enter_f
                     + tiles(hs_ref, 1) + tiles(ps_ref, 1) * enter_b)
            hs_ref[0, c0:c0 + tc, :] = total.reshape(tc, LANES)
        for r in range(SUBLANES):

            def emit(i, _, g_ref=g_ref, o_ref=o_ref, start=start, piece=piece, r=r):
                halves = []
                for m in (2 * i, 2 * i + 1):
                    rows = pl.ds(start + m * (SUBLANES * SUBLANES) + r, SUBLANES, stride=SUBLANES)
                    halves.append(hs_ref[0, rows, :])
                out_rows = pl.ds(r * piece + 2 * SUBLANES * i, 2 * SUBLANES)
                o_ref[out_rows, :] = (g_ref[out_rows, :] * jnp.concatenate(halves, axis=0)).astype(BF16)
                return 0

            static_loop(piece // (2 * SUBLANES), emit, 0)


def _lru_call(layer, u_c, g_c, u_l, g_l, conv_w, conv_b, wg, bg, lam):
    B, C, W = u_c.shape
    S = u_l.shape[1]
    wt = LRU_TILE
    halo = (CONV_WIDTH - 1) * SUBLANES
    seq = lambda n: pl.BlockSpec((None, n, wt), lambda b, j: (b, 0, j))
    return pl.pallas_call(
        _lru_kernel,
        grid=(B, W // wt),
        in_specs=[
            seq(C), seq(C), seq(S), seq(S),
            pl.BlockSpec((None, CONV_WIDTH, wt), lambda b, j: (layer, 0, j)),
            pl.BlockSpec((None, 1, wt), lambda b, j: (layer, 0, j)),
            pl.BlockSpec((None, None, 4, wt, wt), lambda b, j: (layer, j, 0, 0, 0)),
            pl.BlockSpec((None, 4, wt), lambda b, j: (layer, 0, j)),
            pl.BlockSpec((None, 2, wt), lambda b, j: (layer, 0, j)),
        ],
        out_specs=[seq(C), seq(S)],
        out_shape=[jax.ShapeDtypeStruct((B, C, W), BF16), jax.ShapeDtypeStruct((B, S, W), BF16)],
        scratch_shapes=[
            pltpu.VMEM((C + halo, wt), F32),
            pltpu.VMEM((S + halo, wt), F32),
            pltpu.VMEM((C + S, wt), F32),
        ] + [pltpu.VMEM((2, C + S, wt), F32)] * 4,
        compiler_params=_params(2),
        name="rglru",
    )(u_c, g_c, u_l, g_l, conv_w, conv_b, wg, bg, lam)


def _ones_lane(head):
    return V_DIM if head % 2 == 0 else 0


def _attn_kernel(q_ref, *refs):
    o_ref = refs[-1]
    segments = [(refs[2 * i], refs[2 * i + 1]) for i in range((len(refs) - 1) // 2)]
    tq = q_ref.shape[0]

    chunks = []
    for k_ref, v_ref in segments:
        n = k_ref.shape[0]
        kc = min(ATTN_KV_CHUNK, n)
        chunks += [(k_ref, v_ref, slice(c * kc, (c + 1) * kc)) for c in range(n // kc)]

    state = [None] * HEADS_PER_STEP
    for k_ref, v_ref, ks in chunks:
        for hd in range(HEADS_PER_STEP):
            sl = slice(hd * HEAD_PAD, (hd + 1) * HEAD_PAD)
            s = lax.dot_general(q_ref[:, sl], k_ref[ks, sl], (((1,), (1,)), ((), ())),
                                preferred_element_type=F32)
            mc = jnp.max(s, axis=-1, keepdims=True)
            if state[hd] is None:
                m = mc
                acc = _dot(jnp.exp2(s - m).astype(BF16), v_ref[ks, sl])
            else:
                m_old, acc = state[hd]
                m = jnp.maximum(m_old, mc)
                acc = jnp.exp2(m_old - m) * acc + _dot(jnp.exp2(s - m).astype(BF16), v_ref[ks, sl])
            state[hd] = (m, acc)
    outs = []
    for hd, (_, acc) in enumerate(state):
        lane = _ones_lane(hd)
        outs.append(acc / acc[:, lane:lane + 1])
    low = lax.broadcasted_iota(jnp.int32, (tq, LANES), 1) < V_DIM
    for j in range(HEADS_PER_STEP // 2):
        o_ref[:, j * LANES:(j + 1) * LANES] = jnp.where(low, outs[2 * j], outs[2 * j + 1]).astype(BF16)


def _attn_call(q, kv_segments):
    B, Tq, _ = q.shape
    tq = min(ATTN_Q_BLOCK, Tq)
    qw = HEADS_PER_STEP * HEAD_PAD
    vw = HEADS_PER_STEP * V_DIM
    in_specs = [pl.BlockSpec((None, tq, qw), lambda b, p, t: (b, t, p))]
    args = [q]
    for k, v in kv_segments:
        n = k.shape[1]
        in_specs += [pl.BlockSpec((None, n, qw), lambda b, p, t: (b, 0, p))] * 2
        args += [k, v]
    return pl.pallas_call(
        _attn_kernel,
        grid=(B, N_HEADS // HEADS_PER_STEP, Tq // tq),
        in_specs=in_specs,
        out_specs=pl.BlockSpec((None, tq, vw), lambda b, p, t: (b, t, p)),
        out_shape=jax.ShapeDtypeStruct((B, Tq, ATTN_WIDTH), BF16),
        compiler_params=_params(3),
        name="attention",
    )(*args)


def _post_kernel(x_ref, o_ref, m_ref, mod_ref, wo_ref, ng_ref, wfi_ref, wfo_ref, out_ref, *, d_ff):
    x = x_ref[...]
    d_model = x.shape[-1]
    g1 = mod_ref[2:3, :]
    sh2 = mod_ref[3:4, :]
    sc2 = mod_ref[4:5, :]
    g2 = mod_ref[5:6, :]
    y = _dot(o_ref[...], wo_ref[0:ATTN_WIDTH, :]) + _dot(m_ref[...], wo_ref[ATTN_WIDTH:, :])
    x1 = x + g1 * y
    h = x1 * _rms_scale(x1, d_model) * (ng_ref[...] * (1.0 + sc2)) + sh2
    hb = h.astype(BF16)
    acc = jnp.zeros_like(x1)
    for c in range(d_ff // FF_CHUNK):
        lo = c * FF_CHUNK
        gate = _dot(hb, wfi_ref[:, lo:lo + FF_CHUNK])
        up = _dot(hb, wfi_ref[:, d_ff + lo:d_ff + lo + FF_CHUNK])
        act = (gate * jax.nn.sigmoid(gate) * up).astype(BF16)
        acc = acc + _dot(act, wfo_ref[lo:lo + FF_CHUNK, :])
    out_ref[...] = x1 + g2 * acc


def _post_call(layer, x, o, m, mod, mod_row, w_out_b, norm_g, w_ffn_in_b, w_ffn_out_b):
    Bx, Tx, D = x.shape
    tm = ROW_BLOCK
    d_ff = w_ffn_out_b.shape[1]

    def wspec(shape):
        return pl.BlockSpec((None,) + shape, lambda b, t: (layer,) + (0,) * len(shape),
                            pipeline_mode=pl.Buffered(1))

    row = lambda w: pl.BlockSpec((None, tm, w), lambda b, t: (b, t, 0))
    return pl.pallas_call(
        functools.partial(_post_kernel, d_ff=d_ff),
        grid=(Bx, Tx // tm),
        in_specs=[
            row(D),
            row(ATTN_WIDTH),
            row(D - ATTN_WIDTH),
            pl.BlockSpec((None, None, N_MOD, D), lambda b, t: (layer, mod_row(b), 0, 0)),
            wspec((D, D)),
            wspec((1, D)),
            wspec((D, 2 * d_ff)),
            wspec((d_ff, D)),
        ],
        out_specs=row(D),
        out_shape=jax.ShapeDtypeStruct((Bx, Tx, D), F32),
        compiler_params=_params(2),
        name="post_mixer",
    )(x, o, m, mod, w_out_b, norm_g, w_ffn_in_b, w_ffn_out_b)


def _rope_partner():
    r = np.arange(QK_ROPE_DIM)
    i = r % (QK_ROPE_DIM // 2)
    m = QK_ROPE_DIM // 4
    first = i < m
    perm = np.where(first, r + m, r - m)
    sign = np.where(first, -1.0, 1.0).astype(np.float32)
    return perm, sign


def _rope_tables(ang):
    n = ang.shape[0]
    cos_t = jnp.concatenate([jnp.ones((n, QK_NOPE_DIM), F32), jnp.cos(ang),
                             jnp.zeros((n, HEAD_PAD - QK_DIM), F32)], axis=-1)
    sin_t = jnp.concatenate([jnp.zeros((n, QK_NOPE_DIM), F32), jnp.sin(ang),
                             jnp.zeros((n, HEAD_PAD - QK_DIM), F32)], axis=-1)
    return cos_t, sin_t


def _latent_angles(seq):
    m = QK_ROPE_DIM // 4
    inv_freq = ROPE_BASE ** (-jnp.arange(m, dtype=F32) / m)
    pos = jnp.arange(seq, dtype=jnp.int32)
    ang_r = (pos // GRID_W).astype(F32)[:, None] * inv_freq[None, :]
    ang_c = (pos % GRID_W).astype(F32)[:, None] * inv_freq[None, :]
    return jnp.concatenate([ang_r, ang_r, ang_c, ang_c], axis=-1)


def _head_gain(g, perm, scale):
    rope = g[:, QK_NOPE_DIM:]
    return (jnp.concatenate([g, rope[:, perm]], axis=-1) * scale)[:, None, :]


def kernel(x, c, ctx, c_ctx, w_ada, b_ada, norm_mix_g, norm_ffn_g, w_in, q_lora_g, w_uq, kv_lora_g,
           w_ukv, q_norm_g, k_norm_g, conv_w, conv_b, w_rg_a, b_rg_a, w_rg_x, b_rg_x, lru_lambda,
           w_out, w_ffn_in, w_ffn_out):
    B, S, D = x.shape
    C = ctx.shape[1]
    L = w_ada.shape[0]
    d_lru = D - ATTN_WIDTH
    assert S % ROW_BLOCK == 0 and (B * C) % ROW_BLOCK == 0 and S % ATTN_Q_BLOCK == 0
    assert C % LRU_CHUNK == 0 and S % LRU_CHUNK == 0 and S % min(ATTN_KV_CHUNK, S) == 0
    assert d_lru % LRU_TILE == 0 and w_ffn_out.shape[1] % FF_CHUNK == 0
    perm, sign = _rope_partner()

    s0 = Q_LORA_RANK
    s1 = s0 + KV_LORA_RANK
    s2 = s1 + QK_ROPE_DIM
    w_kr = w_in[:, :, s1:s2]
    kr_block = jnp.concatenate([jnp.zeros((L, D, QK_NOPE_DIM), F32), w_kr, w_kr[:, :, perm] * sign], axis=-1)
    w_in_a = jnp.concatenate([w_in[:, :, :s0], kr_block, w_in[:, :, s0:s1], w_in[:, :, s2:]],
                             axis=-1).astype(BF16)

    wq = w_uq.reshape(L, Q_LORA_RANK, N_HEADS, QK_DIM)
    wq_rope = wq[..., QK_NOPE_DIM:]
    w_uq_a = jnp.concatenate([wq, wq_rope[..., perm] * sign], axis=-1)
    w_uq_a = w_uq_a.reshape(L, Q_LORA_RANK, N_HEADS * HEAD_PAD).astype(BF16)

    wkv = w_ukv.reshape(L, KV_LORA_RANK, N_HEADS, QK_NOPE_DIM + V_DIM)
    wk = jnp.concatenate([wkv[..., :QK_NOPE_DIM],
                          jnp.zeros((L, KV_LORA_RANK, N_HEADS, HEAD_PAD - QK_NOPE_DIM), F32)], axis=-1)
    wv = wkv[..., QK_NOPE_DIM:]
    zv = jnp.zeros_like(wv)
    even = (np.arange(N_HEADS) % 2 == 0)[None, None, :, None]
    wv = jnp.where(even, jnp.concatenate([wv, zv], axis=-1), jnp.concatenate([zv, wv], axis=-1))
    w_ukv_a = jnp.concatenate([wk.reshape(L, KV_LORA_RANK, N_HEADS * HEAD_PAD),
                               wv.reshape(L, KV_LORA_RANK, N_HEADS * HEAD_PAD)], axis=-1).astype(BF16)
    v_ones = np.zeros((1, N_HEADS * HEAD_PAD), np.float32)
    for hd in range(N_HEADS):
        v_ones[0, hd * HEAD_PAD + _ones_lane(hd)] = 1.0
    v_ones = jnp.asarray(v_ones)

    gq = _head_gain(q_norm_g, perm, QK_DIM ** -0.5 * LOG2_E)
    gk = _head_gain(k_norm_g, perm, 1.0)
    cos_l, sin_l = _rope_tables(_latent_angles(S))
    cos_c, sin_c = _rope_tables(jnp.zeros((ROW_BLOCK, QK_ROPE_DIM), F32))

    idx = np.arange(2 * HEAD_PAD)
    same_head = (idx[:, None] // HEAD_PAD) == (idx[None, :] // HEAD_PAD)
    head_ones = jnp.asarray(same_head & ((idx[:, None] % HEAD_PAD) < QK_DIM), BF16)

    per_tile = LRU_TILE // (d_lru // LRU_BLOCKS)
    n_tiles = d_lru // LRU_TILE
    wg = jnp.stack([w_rg_a[:, 0], w_rg_x[:, 0], w_rg_a[:, 1], w_rg_x[:, 1]], axis=1)
    wg = wg.reshape(L, 4, n_tiles, per_tile, d_lru // LRU_BLOCKS, d_lru // LRU_BLOCKS)
    eye = jnp.eye(per_tile, dtype=F32)
    wg = jnp.einsum("lghnde,nm->lhgndme", wg, eye)
    wg = (0.5 * wg).reshape(L, n_tiles, 4, LRU_TILE, LRU_TILE).astype(BF16)
    bg = 0.5 * jnp.stack([b_rg_a[:, 0], b_rg_x[:, 0], b_rg_a[:, 1], b_rg_x[:, 1]], axis=1)

    w_out_b = w_out.astype(BF16)
    w_ffn_in_b = w_ffn_in.astype(BF16)
    w_ffn_out_b = w_ffn_out.astype(BF16)

    pad = (-(B + 1)) % MOD_ROWS_PAD
    s_rows = jnp.concatenate([c, c_ctx[None, :], jnp.zeros((pad, D), F32)], axis=0)
    mod = _modulation(s_rows, w_ada, b_ada[:, None, :])
    mod = mod.reshape(L, B + 1 + pad, N_MOD, D)
    lat_row = lambda b: b
    ctx_row = lambda b: B

    mix_g = norm_mix_g[:, None, :]
    ffn_g = norm_ffn_g[:, None, :]
    qlg = q_lora_g[:, None, :]
    kvlg = kv_lora_g[:, None, :]
    cb = conv_b[:, None, :]
    flat = lambda a: a.reshape(1, B * C, a.shape[-1])
    unflat = lambda a: a.reshape(B, C, a.shape[-1])

    xc = ctx
    for l in range(L):
        last = l == L - 1
        q_l, k_l, v_l, u_l, g_l = _pre_call(l, x, mod, lat_row, True, mix_g, w_in_a, qlg, w_uq_a, kvlg,
                                            w_ukv_a, gq, gk, cos_l, sin_l, head_ones, v_ones)
        q_c, k_c, v_c, u_c, g_c = _pre_call(l, flat(xc), mod, ctx_row, False, mix_g, w_in_a, qlg, w_uq_a,
                                            kvlg, w_ukv_a, gq, gk, cos_c, sin_c, head_ones, v_ones)
        k_c, v_c = unflat(k_c), unflat(v_c)
        m_c, m_l = _lru_call(l, unflat(u_c), unflat(g_c), u_l, g_l, conv_w, cb, wg, bg, lru_lambda)
        o_l = _attn_call(q_l, [(k_c, v_c), (k_l, v_l)])
        x = _post_call(l, x, o_l, m_l, mod, lat_row, w_out_b, ffn_g, w_ffn_in_b, w_ffn_out_b)
        if not last:
            o_c = _attn_call(unflat(q_c), [(k_c, v_c)])
            xc = unflat(_post_call(l, flat(xc), flat(o_c), flat(m_c), mod, ctx_row, w_out_b, ffn_g,
                                   w_ffn_in_b, w_ffn_out_b))
    return x
```
